```python
import jax, jax.numpy as jnp
from jax import lax
import numpy as np

D_MODEL = 1024
BATCH = 4
SEQ = 8192
DEPTH = 2
DEC_BATCH = 32
DEC_SEQ = 64
PAST_LEN = 4096

CHUNK = 64
D_MIX = D_MODEL
POOL_WINDOWS = (2, 4, 8, 16)
POOL_GROUPS = len(POOL_WINDOWS)
POOL_WIDTH = D_MIX // 2
POOL_GROUP_DIM = POOL_WIDTH // POOL_GROUPS
POOL_HIST = max(POOL_WINDOWS) - 1
SB_WIDTH = D_MIX - POOL_WIDTH
SB_HEAD_DIM = 64
SB_HEADS = SB_WIDTH // SB_HEAD_DIM
QBLOCK = 128
N_GROUPS = 4
EXPERTS_PER_GROUP = 4
N_EXPERTS = N_GROUPS * EXPERTS_PER_GROUP
TOP_K = 2
D_EXPERT = D_MODEL // 4
ALPHA = (2 * DEPTH) ** 0.25
BETA = (8 * DEPTH) ** -0.25
LN_EPS = 1e-5

kernel_name = "hybrid_pool_stickbreak_hmoe_stream_step"


def _norm(x):
    xf = x.astype(jnp.float32)
    mu = jnp.mean(xf, axis=-1, keepdims=True)
    var = jnp.mean(jnp.square(xf - mu), axis=-1, keepdims=True)
    return (xf - mu) * lax.rsqrt(var + LN_EPS)


def _layer_norm(x, g, b):
    return (_norm(x) * g + b).astype(x.dtype)


def _pool_mixer(u, hist, pos0, w_pool, pool_scale):
    b, n, _ = u.shape
    ue = jnp.concatenate([hist, u], axis=1)
    cs = jnp.cumsum(ue.astype(jnp.float32), axis=1)
    cs = jnp.pad(cs, ((0, 0), (1, 0), (0, 0)))
    pos = pos0 + jnp.arange(n)
    end = cs[:, POOL_HIST + 1:POOL_HIST + 1 + n]
    means = []
    for gi, w in enumerate(POOL_WINDOWS):
        sl = slice(gi * POOL_GROUP_DIM, (gi + 1) * POOL_GROUP_DIM)
        start = cs[:, POOL_HIST + 1 - w:POOL_HIST + 1 - w + n, sl]
        cnt = jnp.minimum(w, pos + 1).astype(jnp.float32)
        means.append((end[..., sl] - start) / cnt[None, :, None])
    pooled = jnp.concatenate(means, axis=-1) - u.astype(jnp.float32)
    pooled = pooled.reshape(b, n, POOL_GROUPS, POOL_GROUP_DIM).astype(u.dtype)
    mixed = jnp.einsum('bngc,gcd->bngd', pooled, w_pool).reshape(b, n, POOL_WIDTH)
    return mixed * pool_scale, ue[:, -POOL_HIST:]


def _stick_breaking(q, k, v, q_pos, k_pos):
    b, nq, h, dh = q.shape
    qb = min(QBLOCK, nq)
    nblk = -(-nq // qb)
    pad = nblk * qb - nq
    qp = jnp.pad(q, ((0, 0), (0, pad), (0, 0), (0, 0)))
    pp = jnp.pad(q_pos, (0, pad))
    qblocks = qp.reshape(b, nblk, qb, h, dh).transpose(1, 0, 2, 3, 4)
    pblocks = pp.reshape(nblk, qb)
    scale = 1.0 / np.sqrt(dh)

    def one_block(args):
        qi, pi = args
        z = jnp.einsum('bqhd,bkhd->bhqk', qi, k).astype(jnp.float32) * scale
        valid = (k_pos[None, :] < pi[:, None])[None, None]
        log_beta = jax.nn.log_sigmoid(z)
        log_1m = jnp.where(valid, jax.nn.log_sigmoid(-z), 0.0)
        suffix = lax.cumsum(log_1m, axis=3, reverse=True) - log_1m
        a = jnp.where(valid, jnp.exp(log_beta + suffix), 0.0)
        return jnp.einsum('bhqk,bkhd->bqhd', a.astype(v.dtype), v)

    out = lax.map(one_block, (qblocks, pblocks))
    out = out.transpose(1, 0, 2, 3, 4).reshape(b, nblk * qb, h, dh)
    return out[:, :nq]


def _moe(hn, w_group, b_group, w_router, b_router, w_up, w_down):
    b, n, d = hn.shape
    t = hn.reshape(b * n, d)
    pg_all = jax.nn.softmax((t @ w_group + b_group).astype(jnp.float32), axis=-1)
    g = jnp.argmax(pg_all, axis=-1)
    pg = jnp.max(pg_all, axis=-1)
    el = (t @ w_router + b_router).astype(jnp.float32).reshape(-1, N_GROUPS, EXPERTS_PER_GROUP)
    el = jnp.take_along_axis(el, g[:, None, None], axis=1)[:, 0]
    pe = jax.nn.softmax(el, axis=-1)
    wv, idx = lax.top_k(pe, TOP_K)
    wv = wv / jnp.sum(wv, axis=-1, keepdims=True) * pg[:, None]
    ids = g[:, None] * EXPERTS_PER_GROUP + idx
    gate = jnp.sum(jax.nn.one_hot(ids, N_EXPERTS, dtype=jnp.float32) * wv[..., None], axis=1)
    hid = jnp.einsum('td,edf->tef', t, w_up)
    a, u = jnp.split(hid, 2, axis=-1)
    act = jax.nn.silu(a) * u * gate.astype(hn.dtype)[..., None]
    y = jnp.einsum('tef,efd->td', act, w_down)
    return y.reshape(b, n, d)


def _layer(x, c, hist_pool, hist_k, hist_v, w_mod, b_mod, w_in, w_pool, pool_scale, w_out,
           ln1_g, ln1_b, ln2_g, ln2_b, w_group, b_group, w_router, b_router, w_up, w_down):
    b, n, _ = x.shape
    past = hist_k.shape[1]
    mod = (c @ w_mod + b_mod)[:, None, :]
    sh1, sc1, g1, sh2, sc2, g2 = jnp.split(mod, 6, axis=-1)
    hn = (_norm(x) * (1 + sc1) + sh1).astype(x.dtype)
    proj = hn @ w_in
    u, q, k, v = jnp.split(proj, [POOL_WIDTH, POOL_WIDTH + SB_WIDTH, POOL_WIDTH + 2 * SB_WIDTH], axis=-1)
    q = q.reshape(b, n, SB_HEADS, SB_HEAD_DIM)
    k = k.reshape(b, n, SB_HEADS, SB_HEAD_DIM)
    v = v.reshape(b, n, SB_HEADS, SB_HEAD_DIM)
    pool_out, new_pool = _pool_mixer(u, hist_pool, past, w_pool, pool_scale)
    k_all = jnp.concatenate([hist_k, k], axis=1)
    v_all = jnp.concatenate([hist_v, v], axis=1)
    q_pos = past + jnp.arange(n)
    k_pos = jnp.arange(past + n)
    attn = _stick_breaking(q, k_all, v_all, q_pos, k_pos).reshape(b, n, SB_WIDTH)
    mix = jnp.concatenate([pool_out, attn], axis=-1) @ w_out
    x = _layer_norm(ALPHA * x + (1 + g1) * mix, ln1_g, ln1_b)
    hn2 = (_norm(x) * (1 + sc2) + sh2).astype(x.dtype)
    ff = _moe(hn2, w_group, b_group, w_router, b_router, w_up, w_down)
    x = _layer_norm(ALPHA * x + (1 + g2) * ff, ln2_g, ln2_b)
    return x, k, v, new_pool


def setup_inputs(seed: int = 0) -> dict:
    key = jax.random.key(seed)
    ks = jax.random.split(key, 24)

    def nrm(k, shape, scale=1.0):
        return jax.random.normal(k, shape, jnp.float32) * scale

    d = D_MODEL
    col_scale = jnp.concatenate([
        jnp.full((POOL_WIDTH,), BETA, jnp.float32),
        jnp.ones((2 * SB_WIDTH,), jnp.float32),
        jnp.full((SB_WIDTH,), BETA, jnp.float32)])
    return {
        'x_prompt': nrm(ks[0], (BATCH, SEQ, d)),
        'x_sample': nrm(ks[1], (DEC_BATCH, DEC_SEQ, d)),
        'c_prompt': nrm(ks[2], (BATCH, d)),
        'c_sample': nrm(ks[3], (DEC_BATCH, d)),
        'cache_k': nrm(ks[4], (DEPTH, DEC_BATCH, PAST_LEN, SB_HEADS, SB_HEAD_DIM)),
        'cache_v': nrm(ks[5], (DEPTH, DEC_BATCH, PAST_LEN, SB_HEADS, SB_HEAD_DIM), BETA),
        'state_pool': nrm(ks[6], (DEPTH, DEC_BATCH, POOL_HIST, POOL_WIDTH), BETA),
        'w_mod': nrm(ks[7], (DEPTH, d, 6 * d), 0.1 * d ** -0.5),
        'b_mod': nrm(ks[8], (DEPTH, 6 * d), 0.02),
        'w_in': nrm(ks[9], (DEPTH, d, POOL_WIDTH + 3 * SB_WIDTH), d ** -0.5) * col_scale,
        'w_pool': nrm(ks[10], (DEPTH, POOL_GROUPS, POOL_GROUP_DIM, POOL_GROUP_DIM), POOL_GROUP_DIM ** -0.5),
        'pool_scale': 1.0 + nrm(ks[11], (DEPTH, POOL_WIDTH), 0.1),
        'w_out': nrm(ks[12], (DEPTH, D_MIX, d), BETA * D_MIX ** -0.5),
        'ln1_g': 1.0 + nrm(ks[13], (DEPTH, d), 0.05),
        'ln1_b': nrm(ks[14], (DEPTH, d), 0.02),
        'ln2_g': 1.0 + nrm(ks[15], (DEPTH, d), 0.05),
        'ln2_b': nrm(ks[16], (DEPTH, d), 0.02),
        'w_group': nrm(ks[17], (DEPTH, d, N_GROUPS), d ** -0.5),
        'b_group': nrm(ks[18], (DEPTH, N_GROUPS), 0.01),
        'w_router': nrm(ks[19], (DEPTH, d, N_EXPERTS), d ** -0.5),
        'b_router': nrm(ks[20], (DEPTH, N_EXPERTS), 0.01),
        'w_up': nrm(ks[21], (DEPTH, N_EXPERTS, d, 2 * D_EXPERT), d ** -0.5),
        'w_down': nrm(ks[22], (DEPTH, N_EXPERTS, D_EXPERT, d), BETA * D_EXPERT ** -0.5),
    }


def reference(x_prompt, x_sample, c_prompt, c_sample, cache_k, cache_v, state_pool,
              w_mod, b_mod, w_in, w_pool, pool_scale, w_out, ln1_g, ln1_b, ln2_g, ln2_b,
              w_group, b_group, w_router, b_router, w_up, w_down):
    params = (w_mod, b_mod, w_in, w_pool, pool_scale, w_out, ln1_g, ln1_b, ln2_g, ln2_b,
              w_group, b_group, w_router, b_router, w_up, w_down)

    bp = x_prompt.shape[0]
    zero_pool = jnp.zeros((bp, POOL_HIST, POOL_WIDTH), x_prompt.dtype)
    empty_kv = jnp.zeros((bp, 0, SB_HEADS, SB_HEAD_DIM), x_prompt.dtype)
    y = x_prompt
    kp, vp, pp = [], [], []
    for l in range(DEPTH):
        y, k_new, v_new, p_new = _layer(y, c_prompt, zero_pool, empty_kv, empty_kv,
                                        *[w[l] for w in params])
        kp.append(k_new); vp.append(v_new); pp.append(p_new)
    y_prompt = y

    y = x_sample
    kd, vd, pd = [], [], []
    for l in range(DEPTH):
        y, k_new, v_new, p_new = _layer(y, c_sample, state_pool[l], cache_k[l], cache_v[l],
                                        *[w[l] for w in params])
        kd.append(k_new); vd.append(v_new); pd.append(p_new)
    y_sample = y

    k_prompt = jnp.stack(kp); v_prompt = jnp.stack(vp); pool_prompt = jnp.stack(pp)
    k_sample = jnp.stack(kd); v_sample = jnp.stack(vd); pool_sample = jnp.stack(pd)
    return (y_prompt, y_sample, k_prompt, v_prompt, pool_prompt, k_sample, v_sample, pool_sample)
```

```python
import functools

import jax
import jax.numpy as jnp
from jax import lax
from jax.experimental import pallas as pl
from jax.experimental.pallas import tpu as pltpu

F32 = jnp.float32
BF16 = jnp.bfloat16

D_MODEL = 1024
POOL_WINDOWS = (2, 4, 8, 16)
POOL_WIDTH = 512
POOL_GROUP_DIM = 128
POOL_HIST = 15
HIST_ROWS = 16
SB_WIDTH = 512
SB_HEADS = 8
SB_HEAD_DIM = 64
N_GROUPS = 4
EXPERTS_PER_GROUP = 4
N_EXPERTS = 16
D_EXPERT = 256
ROUTER_LANES = 128
DEPTH = 2
ALPHA = (2 * DEPTH) ** 0.25
LN_EPS = 1e-5
QK_SCALE = 1.0 / 8.0
LOG_UNDERFLOW = -104.0
NEG_BIG = -1e30
VMEM_LIMIT = 56 * 1024 * 1024


def _norm(x):
    mu = jnp.mean(x, axis=-1, keepdims=True)
    xc = x - mu
    var = jnp.mean(xc * xc, axis=-1, keepdims=True)
    return xc * lax.rsqrt(var + LN_EPS)


def _bdot(a, b):
    return jnp.dot(a, b, preferred_element_type=F32)


def _mod_kernel(c_ref, w_ref, b_ref, o_ref):
    o_ref[0] = _bdot(c_ref[...].astype(BF16), w_ref[0]) + b_ref[0]


def _mod_call(c_all, w_mod, b_mod):
    depth, d, n6 = w_mod.shape
    bc = c_all.shape[0]
    tn = 1536
    return pl.pallas_call(
        _mod_kernel,
        grid=(depth, n6 // tn),
        in_specs=[
            pl.BlockSpec((bc, d), lambda l, j: (0, 0)),
            pl.BlockSpec((1, d, tn), lambda l, j: (l, 0, j)),
            pl.BlockSpec((1, 1, tn), lambda l, j: (l, 0, j)),
        ],
        out_specs=pl.BlockSpec((1, bc, tn), lambda l, j: (l, 0, j)),
        out_shape=jax.ShapeDtypeStruct((depth, bc, n6), F32),
        name="mod",
    )(c_all, w_mod, b_mod.reshape(depth, 1, n6))


def _inproj_kernel(x_ref, mod_ref, hist_ref, w_in_ref, w_pool_ref, ps_ref,
                   q_ref, k_ref, v_ref, kb_ref, vb_ref, po_ref, np_ref, ue_ref, *, tm, pos0):
    j = pl.program_id(1)
    x = x_ref[0]
    sh1 = mod_ref[0, 0:1, :]
    sc1 = mod_ref[0, 1:2, :]
    hn = (_norm(x) * (1.0 + sc1) + sh1).astype(BF16)
    proj = _bdot(hn, w_in_ref[...])
    u = proj[:, 0:POOL_WIDTH]
    q = proj[:, POOL_WIDTH:POOL_WIDTH + SB_WIDTH]
    k = proj[:, POOL_WIDTH + SB_WIDTH:POOL_WIDTH + 2 * SB_WIDTH]
    v = proj[:, POOL_WIDTH + 2 * SB_WIDTH:]
    q_ref[0] = (q * QK_SCALE).astype(BF16)
    k_ref[0] = k
    v_ref[0] = v
    kb_ref[0] = k.astype(BF16)
    vb_ref[0] = v.astype(BF16)

    @pl.when(j == 0)
    def _():
        ue_ref[0:HIST_ROWS, :] = hist_ref[0]

    ue_ref[HIST_ROWS:HIST_ROWS + tm, :] = u
    pos = pos0 + j * tm + lax.broadcasted_iota(jnp.int32, (tm, 1), 0)
    outs = []
    for g, w in enumerate(POOL_WINDOWS):
        c0 = g * POOL_GROUP_DIM
        c1 = c0 + POOL_GROUP_DIM
        acc = ue_ref[HIST_ROWS:HIST_ROWS + tm, c0:c1]
        for dlt in range(1, w):
            acc = acc + ue_ref[HIST_ROWS - dlt:HIST_ROWS - dlt + tm, c0:c1]
        cnt = jnp.minimum(w, pos + 1).astype(F32)
        pooled = acc / cnt - u[:, c0:c1]
        mixed = _bdot(pooled.astype(BF16), w_pool_ref[g])
        outs.append(mixed * ps_ref[:, c0:c1])
    po_ref[0] = jnp.concatenate(outs, axis=1).astype(BF16)

    tail = ue_ref[tm:tm + HIST_ROWS, :]
    ue_ref[0:HIST_ROWS, :] = tail

    @pl.when(j == pl.num_programs(1) - 1)
    def _():
        np_ref[0] = tail


def _inproj_call(x, mod, hist16, w_in, w_pool, pool_scale, *, tm, pos0):
    b, n, d = x.shape
    nproj = w_in.shape[1]
    tok = lambda width: pl.BlockSpec((1, tm, width), lambda bi, j: (bi, j, 0))
    per_b = lambda rows, width: pl.BlockSpec((1, rows, width), lambda bi, j: (bi, 0, 0))
    const2 = lambda s: pl.BlockSpec(s, lambda bi, j: (0, 0))
    return pl.pallas_call(
        functools.partial(_inproj_kernel, tm=tm, pos0=pos0),
        grid=(b, n // tm),
        in_specs=[
            tok(d),
            per_b(6, d),
            per_b(HIST_ROWS, POOL_WIDTH),
            const2((d, nproj)),
            pl.BlockSpec(w_pool.shape, lambda bi, j: (0, 0, 0)),
            const2((1, POOL_WIDTH)),
        ],
        out_specs=[tok(SB_WIDTH), tok(SB_WIDTH), tok(SB_WIDTH), tok(SB_WIDTH), tok(SB_WIDTH),
                   tok(POOL_WIDTH), per_b(HIST_ROWS, POOL_WIDTH)],
        out_shape=[
            jax.ShapeDtypeStruct((b, n, SB_WIDTH), BF16),
            jax.ShapeDtypeStruct((b, n, SB_WIDTH), F32),
            jax.ShapeDtypeStruct((b, n, SB_WIDTH), F32),
            jax.ShapeDtypeStruct((b, n, SB_WIDTH), BF16),
            jax.ShapeDtypeStruct((b, n, SB_WIDTH), BF16),
            jax.ShapeDtypeStruct((b, n, POOL_WIDTH), BF16),
            jax.ShapeDtypeStruct((b, HIST_ROWS, POOL_WIDTH), F32),
        ],
        scratch_shapes=[pltpu.VMEM((tm + HIST_ROWS, POOL_WIDTH), F32)],
        compiler_params=pltpu.CompilerParams(
            dimension_semantics=("arbitrary", "arbitrary"), vmem_limit_bytes=VMEM_LIMIT),
        name="inproj",
    )(x, mod, hist16, w_in, w_pool, pool_scale)


def _suffix_matrix(tk):
    r = lax.broadcasted_iota(jnp.int32, (tk, tk + 128), 0)
    c = lax.broadcasted_iota(jnp.int32, (tk, tk + 128), 1)
    return jnp.where((r > c) | (c >= tk), 1.0, 0.0).astype(BF16)


def _sb_block(qh, kblk, vblk, carry, mask, m_ext):
    tk = kblk.shape[0]
    z = lax.dot_general(qh, kblk, (((1,), (1,)), ((), ())), preferred_element_type=F32)
    l1p = jnp.log(1.0 + jnp.exp(-jnp.abs(z)))
    log_beta = jnp.minimum(z, 0.0) - l1p
    log_1m = log_beta - z
    if mask is not None:
        log_1m = jnp.where(mask, log_1m, 0.0)
    hi = log_1m.astype(BF16)
    lo = (log_1m - hi.astype(F32)).astype(BF16)
    s = _bdot(hi, m_ext) + _bdot(lo, m_ext)
    arg = log_beta + s[:, 0:tk]
    tot = s[:, tk:]
    if carry is not None:
        arg = arg + jnp.concatenate([carry] * (tk // 128), axis=1)
    a = jnp.exp(arg)
    if mask is not None:
        a = jnp.where(mask, a, 0.0)
    return _bdot(a.astype(BF16), vblk), tot


def _all_underflowed(carry):
    return (jnp.max(carry) < LOG_UNDERFLOW).astype(jnp.int32)


def _attn_prompt_kernel(q_ref, k_ref, v_ref, o_ref, acc_ref, carry_ref, *, tq):
    i = pl.program_id(1)
    tk = tq
    m_ext = _suffix_matrix(tk)
    row = lax.broadcasted_iota(jnp.int32, (tq, tk), 0)
    col = lax.broadcasted_iota(jnp.int32, (tq, tk), 1)
    diag_mask = col < row
    start = pl.multiple_of(i * tq, tq)
    for h in range(SB_HEADS):
        sl = slice(h * SB_HEAD_DIM, (h + 1) * SB_HEAD_DIM)
        qh = q_ref[0, :, sl]
        pv, tot = _sb_block(qh, k_ref[0, pl.ds(start, tk), sl], v_ref[0, pl.ds(start, tk), sl],
                            None, diag_mask, m_ext)
        acc_ref[h] = pv
        carry_ref[...] = tot

        def body(st, qh=qh, sl=sl, h=h):
            jb, _ = st
            s0 = pl.multiple_of(jb * tk, tk)
            pv, tot = _sb_block(qh, k_ref[0, pl.ds(s0, tk), sl], v_ref[0, pl.ds(s0, tk), sl],
                                carry_ref[...], None, m_ext)
            acc_ref[h] += pv
            c = carry_ref[...] + tot
            carry_ref[...] = c
            return jb - 1, _all_underflowed(c)

        lax.while_loop(lambda st: (st[0] >= 0) & (st[1] == 0), body, (i - 1, jnp.int32(0)))
        o_ref[0, :, sl] = acc_ref[h].astype(BF16)


def _attn_prompt_call(q, kb, vb, *, tq):
    b, n, w = q.shape
    qspec = pl.BlockSpec((1, tq, w), lambda bi, i: (bi, i, 0))
    kvspec = pl.BlockSpec((1, n, w), lambda bi, i: (bi, 0, 0))
    return pl.pallas_call(
        functools.partial(_attn_prompt_kernel, tq=tq),
        grid=(b, n // tq),
        in_specs=[qspec, kvspec, kvspec],
        out_specs=qspec,
        out_shape=jax.ShapeDtypeStruct((b, n, w), BF16),
        scratch_shapes=[pltpu.VMEM((SB_HEADS, tq, SB_HEAD_DIM), F32),
                        pltpu.VMEM((tq, 128), F32)],
        compiler_params=pltpu.CompilerParams(
            dimension_semantics=("arbitrary", "arbitrary"), vmem_limit_bytes=VMEM_LIMIT),
        name="attn_prompt",
    )(q, kb, vb)


def _attn_sample_kernel(q_ref, k_ref, v_ref, ck_ref, cv_ref, o_ref, acc_ref, carry_ref, *, tq, tk, nblk):
    m_new = _suffix_matrix(tq)
    m_ext = _suffix_matrix(tk)
    row = lax.broadcasted_iota(jnp.int32, (tq, tq), 0)
    col = lax.broadcasted_iota(jnp.int32, (tq, tq), 1)
    diag_mask = col < row
    for h in range(SB_HEADS):
        sl = slice(h * SB_HEAD_DIM, (h + 1) * SB_HEAD_DIM)
        qh = q_ref[0, :, sl]
        pv, tot = _sb_block(qh, k_ref[0, :, sl], v_ref[0, :, sl], None, diag_mask, m_new)
        acc_ref[h] = pv
        carry_ref[...] = tot

        def body(st, qh=qh, sl=sl, h=h):
            jb, _ = st
            s0 = pl.multiple_of(jb * tk, tk)
            kblk = ck_ref[0, 0, pl.ds(s0, tk), sl].astype(BF16)
            vblk = cv_ref[0, 0, pl.ds(s0, tk), sl].astype(BF16)
            pv, tot = _sb_block(qh, kblk, vblk, carry_ref[...], None, m_ext)
            acc_ref[h] += pv
            c = carry_ref[...] + tot
            carry_ref[...] = c
            return jb - 1, _all_underflowed(c)

        lax.while_loop(lambda st: (st[0] >= 0) & (st[1] == 0), body,
                       (jnp.int32(nblk - 1), jnp.int32(0)))
        o_ref[0, :, sl] = acc_ref[h].astype(BF16)


def _attn_sample_call(q, kb, vb, cache_k, cache_v, layer, *, tk):
    b, n, w = q.shape
    past = cache_k.shape[2]
    tk = min(tk, past)
    newspec = pl.BlockSpec((1, n, w), lambda bi: (bi, 0, 0))
    cspec = pl.BlockSpec((1, 1, past, w), lambda bi: (layer, bi, 0, 0))
    return pl.pallas_call(
        functools.partial(_attn_sample_kernel, tq=n, tk=tk, nblk=past // tk),
        grid=(b,),
        in_specs=[newspec, newspec, newspec, cspec, cspec],
        out_specs=newspec,
        out_shape=jax.ShapeDtypeStruct((b, n, w), BF16),
        scratch_shapes=[pltpu.VMEM((SB_HEADS, n, SB_HEAD_DIM), F32),
                        pltpu.VMEM((n, 128), F32)],
        compiler_params=pltpu.CompilerParams(
            dimension_semantics=("arbitrary",), vmem_limit_bytes=VMEM_LIMIT),
        name="attn_sample",
    )(q, kb, vb, cache_k, cache_v)


def _outproj_kernel(x_ref, po_ref, at_ref, mod_ref, w_out_ref, g_ref, b_ref, x1_ref, mix_ref, *, bt, tn):
    for s in range(bt):
        rows = slice(s * tn, (s + 1) * tn)
        mix_ref[rows, :] = (_bdot(po_ref[s], w_out_ref[0:POOL_WIDTH, :])
                            + _bdot(at_ref[s], w_out_ref[POOL_WIDTH:, :]))
    for s in range(bt):
        rows = slice(s * tn, (s + 1) * tn)
        g1 = mod_ref[s, 2:3, :]
        y = ALPHA * x_ref[s] + (1.0 + g1) * mix_ref[rows, :]
        x1_ref[s] = _norm(y) * g_ref[...] + b_ref[...]


def _outproj_call(x, po, at, mod, w_out, ln_g, ln_b, *, bt, tn):
    b, n, d = x.shape
    tok = lambda width: pl.BlockSpec((bt, tn, width), lambda bi, j: (bi, j, 0))
    const2 = lambda s: pl.BlockSpec(s, lambda bi, j: (0, 0))
    return pl.pallas_call(
        functools.partial(_outproj_kernel, bt=bt, tn=tn),
        grid=(b // bt, n // tn),
        in_specs=[tok(d), tok(POOL_WIDTH), tok(SB_WIDTH),
                  pl.BlockSpec((bt, 6, d), lambda bi, j: (bi, 0, 0)),
                  const2(w_out.shape), const2((1, d)), const2((1, d))],
        out_specs=tok(d),
        out_shape=jax.ShapeDtypeStruct((b, n, d), F32),
        scratch_shapes=[pltpu.VMEM((bt * tn, d), F32)],
        compiler_params=pltpu.CompilerParams(
            dimension_semantics=("arbitrary", "arbitrary"), vmem_limit_bytes=VMEM_LIMIT),
        name="outproj",
    )(x, po, at, mod, w_out, ln_g, ln_b)


def _route(logits):
    col = lax.broadcasted_iota(jnp.int32, logits.shape, 1)
    colf = col.astype(F32)
    is_group = col < N_GROUPS
    gl = jnp.where(is_group, logits, NEG_BIG)
    gmax = jnp.max(gl, axis=1, keepdims=True)
    gsum = jnp.sum(jnp.where(is_group, jnp.exp(gl - gmax), 0.0), axis=1, keepdims=True)
    pg = 1.0 / gsum
    gidx = jnp.min(jnp.where(gl == gmax, colf, 1e9), axis=1, keepdims=True)
    egrp = ((col - N_GROUPS) >> 2).astype(F32)
    in_group = (col >= N_GROUPS) & (col < N_GROUPS + N_EXPERTS) & (egrp == gidx)
    el = jnp.where(in_group, logits, NEG_BIG)
    m1 = jnp.max(el, axis=1, keepdims=True)
    i1 = jnp.min(jnp.where(el == m1, colf, 1e9), axis=1, keepdims=True)
    el2 = jnp.where(colf == i1, NEG_BIG, el)
    m2 = jnp.max(el2, axis=1, keepdims=True)
    i2 = jnp.min(jnp.where(el2 == m2, colf, 1e9), axis=1, keepdims=True)
    r = jnp.exp(m2 - m1)
    w1 = pg / (1.0 + r)
    w2 = pg * r / (1.0 + r)
    return jnp.where(colf == i1, w1, 0.0) + jnp.where(colf == i2, w2, 0.0)


def _moe_kernel(x1_ref, mod_ref, wr_ref, br_ref, wup_ref, wdn_ref, g_ref, b_ref, o_ref,
                hn_ref, gate_ref, acc_ref, *, bt, tn):
    e = pl.program_id(2)

    @pl.when(e == 0)
    def _():
        for s in range(bt):
            sh2 = mod_ref[s, 3:4, :]
            sc2 = mod_ref[s, 4:5, :]
            hn_ref[s * tn:(s + 1) * tn, :] = (_norm(x1_ref[s]) * (1.0 + sc2) + sh2).astype(BF16)
        logits = _bdot(hn_ref[...], wr_ref[...]) + br_ref[...]
        gate_ref[...] = _route(logits)
        acc_ref[...] = jnp.zeros_like(acc_ref)

    hid = _bdot(hn_ref[...], wup_ref[0])
    a = hid[:, 0:D_EXPERT]
    u = hid[:, D_EXPERT:]
    gate = gate_ref[...]
    col = lax.broadcasted_iota(jnp.int32, gate.shape, 1)
    gcol = jnp.sum(jnp.where(col == e + N_GROUPS, gate, 0.0), axis=1, keepdims=True)
    act = a * (1.0 / (1.0 + jnp.exp(-a))) * u * gcol
    acc_ref[...] += _bdot(act.astype(BF16), wdn_ref[0])

    @pl.when(e == pl.num_programs(2) - 1)
    def _():
        for s in range(bt):
            g2 = mod_ref[s, 5:6, :]
            y = ALPHA * x1_ref[s] + (1.0 + g2) * acc_ref[s * tn:(s + 1) * tn, :]
            o_ref[s] = _norm(y) * g_ref[...] + b_ref[...]


def _moe_call(x1, mod, w_r, b_r, w_up, w_down, ln_g, ln_b, *, bt, tn):
    b, n, d = x1.shape
    ne = w_up.shape[0]
    tm = bt * tn
    tok = pl.BlockSpec((bt, tn, d), lambda bi, j, e: (bi, j, 0))
    const2 = lambda s: pl.BlockSpec(s, lambda bi, j, e: (0, 0))
    return pl.pallas_call(
        functools.partial(_moe_kernel, bt=bt, tn=tn),
        grid=(b // bt, n // tn, ne),
        in_specs=[tok,
                  pl.BlockSpec((bt, 6, d), lambda bi, j, e: (bi, 0, 0)),
                  const2(w_r.shape), const2((1, ROUTER_LANES)),
                  pl.BlockSpec((1,) + w_up.shape[1:], lambda bi, j, e: (e, 0, 0)),
                  pl.BlockSpec((1,) + w_down.shape[1:], lambda bi, j, e: (e, 0, 0)),
                  const2((1, d)), const2((1, d))],
        out_specs=tok,
        out_shape=jax.ShapeDtypeStruct((b, n, d), F32),
        scratch_shapes=[pltpu.VMEM((tm, d), BF16),
                        pltpu.VMEM((tm, ROUTER_LANES), F32),
                        pltpu.VMEM((tm, d), F32)],
        compiler_params=pltpu.CompilerParams(
            dimension_semantics=("arbitrary", "arbitrary", "arbitrary"), vmem_limit_bytes=VMEM_LIMIT),
        name="moe",
    )(x1, mod, w_r, b_r, w_up, w_down, ln_g, ln_b)


def _pick(n, pref):
    t = min(n, pref)
    while n % t:
        t //= 2
    return t


def kernel(x_prompt, x_sample, c_prompt, c_sample, cache_k, cache_v, state_pool, w_mod, b_mod, w_in, w_pool, pool_scale, w_out, ln1_g, ln1_b, ln2_g, ln2_b, w_group, b_group, w_router, b_router, w_up, w_down):
    depth = w_mod.shape[0]
    bp, seq, d = x_prompt.shape
    bs, dseq, _ = x_sample.shape
    past = cache_k.shape[2]

    w_mod_b = w_mod.astype(BF16)
    w_in_b = w_in.astype(BF16)
    w_pool_b = w_pool.astype(BF16)
    w_out_b = w_out.astype(BF16)
    w_up_b = w_up.astype(BF16)
    w_down_b = w_down.astype(BF16)
    pad = jnp.zeros((depth, d, ROUTER_LANES - N_GROUPS - N_EXPERTS), F32)
    w_r = jnp.concatenate([w_group, w_router, pad], axis=-1).astype(BF16)
    b_r = jnp.concatenate([b_group, b_router, pad[:, 0, :]], axis=-1).reshape(depth, 1, ROUTER_LANES)
    cache_k4 = cache_k.reshape(depth, bs, past, SB_WIDTH)
    cache_v4 = cache_v.reshape(depth, bs, past, SB_WIDTH)

    mod_all = _mod_call(jnp.concatenate([c_prompt, c_sample], axis=0), w_mod_b, b_mod)
    mod_p = mod_all[:, :bp].reshape(depth, bp, 6, d)
    mod_s = mod_all[:, bp:].reshape(depth, bs, 6, d)

    tm_p = _pick(seq, 512)
    tq_p = _pick(seq, 256)
    tn_p = _pick(seq, 1024)
    bt_s = _pick(bs, max(1, 1024 // dseq))

    def run_layer(l, x, mod, hist16, pos0, is_prompt):
        q, k, v, kb, vb, po, new_pool = _inproj_call(
            x, mod, hist16, w_in_b[l], w_pool_b[l], pool_scale[l].reshape(1, POOL_WIDTH),
            tm=tm_p if is_prompt else dseq, pos0=pos0)
        if is_prompt:
            at = _attn_prompt_call(q, kb, vb, tq=tq_p)
            bt, tn = 1, tn_p
        else:
            at = _attn_sample_call(q, kb, vb, cache_k4, cache_v4, l, tk=256)
            bt, tn = bt_s, dseq
        x1 = _outproj_call(x, po, at, mod, w_out_b[l], ln1_g[l].reshape(1, d), ln1_b[l].reshape(1, d),
                           bt=bt, tn=tn)
        x2 = _moe_call(x1, mod, w_r[l], b_r[l], w_up_b[l], w_down_b[l],
                       ln2_g[l].reshape(1, d), ln2_b[l].reshape(1, d), bt=bt, tn=tn)
        b, n = x.shape[0], x.shape[1]
        return (x2, k.reshape(b, n, SB_HEADS, SB_HEAD_DIM), v.reshape(b, n, SB_HEADS, SB_HEAD_DIM),
                new_pool[:, HIST_ROWS - POOL_HIST:])

    y = x_prompt
    kp, vp, pp = [], [], []
    zero_hist = jnp.zeros((bp, HIST_ROWS, POOL_WIDTH), F32)
    for l in range(depth):
        y, k_new, v_new, p_new = run_layer(l, y, mod_p[l], zero_hist, 0, True)
        kp.append(k_new); vp.append(v_new); pp.append(p_new)
    y_prompt = y

    y = x_sample
    kd, vd, pd = [], [], []
    hist_s = jnp.pad(state_pool, ((0, 0), (0, 0), (HIST_ROWS - POOL_HIST, 0), (0, 0)))
    for l in range(depth):
        y, k_new, v_new, p_new = run_layer(l, y, mod_s[l], hist_s[l], past, False)
        kd.append(k_new); vd.append(v_new); pd.append(p_new)
    y_sample = y

    return (y_prompt, y_sample, jnp.stack(kp), jnp.stack(vp), jnp.stack(pp),
            jnp.stack(kd), jnp.stack(vd), jnp.stack(pd))
```

```python
import functools

import jax
import jax.numpy as jnp
from jax import lax
from jax.experimental import pallas as pl
from jax.experimental.pallas import tpu as pltpu

F32 = jnp.float32
BF16 = jnp.bfloat16

D_MODEL = 1024
POOL_WINDOWS = (2, 4, 8, 16)
POOL_WIDTH = 512
POOL_GROUP_DIM = 128
POOL_HIST = 15
HIST_ROWS = 16
SB_WIDTH = 512
SB_HEADS = 8
SB_HEAD_DIM = 64
N_GROUPS = 4
EXPERTS_PER_GROUP = 4
N_EXPERTS = 16
D_EXPERT = 256
ROUTER_LANES = 128
DEPTH = 2
ALPHA = (2 * DEPTH) ** 0.25
LN_EPS = 1e-5
QK_SCALE = 1.0 / 8.0
LOG_UNDERFLOW = -104.0
NEG_BIG = -1e30
VMEM_LIMIT = 56 * 1024 * 1024


def _norm(x):
    mu = jnp.mean(x, axis=-1, keepdims=True)
    xc = x - mu
    var = jnp.mean(xc * xc, axis=-1, keepdims=True)
    return xc * lax.rsqrt(var + LN_EPS)


def _bdot(a, b):
    return jnp.dot(a, b, preferred_element_type=F32)


def _mod_kernel(c_ref, w_ref, b_ref, o_ref):
    o_ref[0] = _bdot(c_ref[...].astype(BF16), w_ref[0]) + b_ref[0]


def _mod_call(c_all, w_mod, b_mod):
    depth, d, n6 = w_mod.shape
    bc = c_all.shape[0]
    tn = 1536
    return pl.pallas_call(
        _mod_kernel,
        grid=(depth, n6 // tn),
        in_specs=[
            pl.BlockSpec((bc, d), lambda l, j: (0, 0)),
            pl.BlockSpec((1, d, tn), lambda l, j: (l, 0, j)),
            pl.BlockSpec((1, 1, tn), lambda l, j: (l, 0, j)),
        ],
        out_specs=pl.BlockSpec((1, bc, tn), lambda l, j: (l, 0, j)),
        out_shape=jax.ShapeDtypeStruct((depth, bc, n6), F32),
        name="mod",
    )(c_all, w_mod, b_mod.reshape(depth, 1, n6))


def _inproj_kernel(x_ref, mod_ref, hist_ref, w_in_ref, w_pool_ref, ps_ref,
                   q_ref, k_ref, v_ref, kb_ref, vb_ref, po_ref, np_ref, ue_ref, *, tm, pos0):
    j = pl.program_id(1)
    x = x_ref[0]
    sh1 = mod_ref[0, 0:1, :]
    sc1 = mod_ref[0, 1:2, :]
    hn = (_norm(x) * (1.0 + sc1) + sh1).astype(BF16)
    proj = _bdot(hn, w_in_ref[...])
    u = proj[:, 0:POOL_WIDTH]
    q = proj[:, POOL_WIDTH:POOL_WIDTH + SB_WIDTH]
    k = proj[:, POOL_WIDTH + SB_WIDTH:POOL_WIDTH + 2 * SB_WIDTH]
    v = proj[:, POOL_WIDTH + 2 * SB_WIDTH:]
    q_ref[0] = (q * QK_SCALE).astype(BF16)
    k_ref[0] = k
    v_ref[0] = v
    kb_ref[0] = k.astype(BF16)
    vb_ref[0] = v.astype(BF16)

    @pl.when(j == 0)
    def _():
        ue_ref[0:HIST_ROWS, :] = hist_ref[0]

    ue_ref[HIST_ROWS:HIST_ROWS + tm, :] = u
    pos = pos0 + j * tm + lax.broadcasted_iota(jnp.int32, (tm, 1), 0)
    outs = []
    for g, w in enumerate(POOL_WINDOWS):
        c0 = g * POOL_GROUP_DIM
        c1 = c0 + POOL_GROUP_DIM
        acc = ue_ref[HIST_ROWS:HIST_ROWS + tm, c0:c1]
        for dlt in range(1, w):
            acc = acc + ue_ref[HIST_ROWS - dlt:HIST_ROWS - dlt + tm, c0:c1]
        cnt = jnp.minimum(w, pos + 1).astype(F32)
        pooled = acc / cnt - u[:, c0:c1]
        mixed = _bdot(pooled.astype(BF16), w_pool_ref[g])
        outs.append(mixed * ps_ref[:, c0:c1])
    po_ref[0] = jnp.concatenate(outs, axis=1).astype(BF16)

    tail = ue_ref[tm:tm + HIST_ROWS, :]
    ue_ref[0:HIST_ROWS, :] = tail

    @pl.when(j == pl.num_programs(1) - 1)
    def _():
        np_ref[0] = tail


def _inproj_call(x, mod, hist16, w_in, w_pool, pool_scale, *, tm, pos0):
    b, n, d = x.shape
    nproj = w_in.shape[1]
    tok = lambda width: pl.BlockSpec((1, tm, width), lambda bi, j: (bi, j, 0))
    per_b = lambda rows, width: pl.BlockSpec((1, rows, width), lambda bi, j: (bi, 0, 0))
    const2 = lambda s: pl.BlockSpec(s, lambda bi, j: (0, 0))
    return pl.pallas_call(
        functools.partial(_inproj_kernel, tm=tm, pos0=pos0),
        grid=(b, n // tm),
        in_specs=[
            tok(d),
            per_b(6, d),
            per_b(HIST_ROWS, POOL_WIDTH),
            const2((d, nproj)),
            pl.BlockSpec(w_pool.shape, lambda bi, j: (0, 0, 0)),
            const2((1, POOL_WIDTH)),
        ],
        out_specs=[tok(SB_WIDTH), tok(SB_WIDTH), tok(SB_WIDTH), tok(SB_WIDTH), tok(SB_WIDTH),
                   tok(POOL_WIDTH), per_b(HIST_ROWS, POOL_WIDTH)],
        out_shape=[
            jax.ShapeDtypeStruct((b, n, SB_WIDTH), BF16),
            jax.ShapeDtypeStruct((b, n, SB_WIDTH), F32),
            jax.ShapeDtypeStruct((b, n, SB_WIDTH), F32),
            jax.ShapeDtypeStruct((b, n, SB_WIDTH), BF16),
            jax.ShapeDtypeStruct((b, n, SB_WIDTH), BF16),
            jax.ShapeDtypeStruct((b, n, POOL_WIDTH), BF16),
            jax.ShapeDtypeStruct((b, HIST_ROWS, POOL_WIDTH), F32),
        ],
        scratch_shapes=[pltpu.VMEM((tm + HIST_ROWS, POOL_WIDTH), F32)],
        compiler_params=pltpu.CompilerParams(
            dimension_semantics=("arbitrary", "arbitrary"), vmem_limit_bytes=VMEM_LIMIT),
        name="inproj",
    )(x, mod, hist16, w_in, w_pool, pool_scale)


def _suffix_matrix(tk):
    r = lax.broadcasted_iota(jnp.int32, (2 * tk, tk), 0)
    c = lax.broadcasted_iota(jnp.int32, (2 * tk, tk), 1)
    r = jnp.where(r >= tk, r - tk, r)
    return jnp.where(r > c, 1.0, 0.0).astype(BF16)


def _sb_block(qh, kblk, vblk, carry, mask, m2):
    tk = kblk.shape[0]
    z = lax.dot_general(qh, kblk, (((1,), (1,)), ((), ())), preferred_element_type=F32)
    l1p = jnp.log(1.0 + jnp.exp(-jnp.abs(z)))
    log_beta = jnp.minimum(z, 0.0) - l1p
    log_1m = log_beta - z
    if mask is not None:
        log_1m = jnp.where(mask, log_1m, 0.0)
    hi = log_1m.astype(BF16)
    lo = (log_1m - hi.astype(F32)).astype(BF16)
    if tk % 128 == 0:
        suffix = _bdot(jnp.concatenate([hi, lo], axis=1), m2)
    else:
        suffix = _bdot(hi, m2[0:tk]) + _bdot(lo, m2[0:tk])
    arg = log_beta + suffix
    if carry is not None:
        arg = arg + (jnp.concatenate([carry] * (tk // 128), axis=1) if tk > 128 else carry)
    a = jnp.exp(arg)
    if mask is not None:
        a = jnp.where(mask, a, 0.0)
    return _bdot(a.astype(BF16), vblk), jnp.sum(log_1m, axis=1, keepdims=True)


def _sb_heads(q_ref, get_kv, acc_ref, carry_ref, mask, m2, first):
    tq = acc_ref.shape[1]
    for h in range(SB_HEADS):
        sl = slice(h * SB_HEAD_DIM, (h + 1) * SB_HEAD_DIM)
        kblk, vblk = get_kv(h, sl)
        pv, tot = _sb_block(q_ref[0, :, sl], kblk, vblk, None if first else carry_ref[h], mask, m2)
        tot = jnp.broadcast_to(tot, (tq, 128))
        if first:
            acc_ref[h] = pv
            carry_ref[h] = tot
        else:
            acc_ref[h] += pv
            carry_ref[h] += tot


def _sb_heads_batched(q_ref, get_kv, acc_ref, carry_ref, hl_ref, lb_ref, mask, m2, first):
    tq = acc_ref.shape[1]
    tk = m2.shape[1]
    tots = []
    for h in range(SB_HEADS):
        sl = slice(h * SB_HEAD_DIM, (h + 1) * SB_HEAD_DIM)
        kblk, _ = get_kv(h, sl)
        z = lax.dot_general(q_ref[0, :, sl], kblk, (((1,), (1,)), ((), ())), preferred_element_type=F32)
        l1p = jnp.log(1.0 + jnp.exp(-jnp.abs(z)))
        log_beta = jnp.minimum(z, 0.0) - l1p
        log_1m = log_beta - z
        if mask is not None:
            log_1m = jnp.where(mask, log_1m, 0.0)
        hi = log_1m.astype(BF16)
        lo = (log_1m - hi.astype(F32)).astype(BF16)
        hl_ref[h * tq:(h + 1) * tq, :] = jnp.concatenate([hi, lo], axis=1)
        lb_ref[h] = log_beta
        tots.append(jnp.broadcast_to(jnp.sum(log_1m, axis=1, keepdims=True), (tq, 128)))
    suffix = _bdot(hl_ref[...], m2)
    for h in range(SB_HEADS):
        sl = slice(h * SB_HEAD_DIM, (h + 1) * SB_HEAD_DIM)
        _, vblk = get_kv(h, sl)
        arg = lb_ref[h] + suffix[h * tq:(h + 1) * tq]
        if not first:
            arg = arg + jnp.concatenate([carry_ref[h]] * (tk // 128), axis=1)
        a = jnp.exp(arg)
        if mask is not None:
            a = jnp.where(mask, a, 0.0)
        pv = _bdot(a.astype(BF16), vblk)
        if first:
            acc_ref[h] = pv
            carry_ref[h] = tots[h]
        else:
            acc_ref[h] += pv
            carry_ref[h] += tots[h]


def _all_underflowed(carry_ref):
    return (jnp.max(carry_ref[...]) < LOG_UNDERFLOW).astype(jnp.int32)


def _store_heads(o_ref, acc_ref):
    for h in range(SB_HEADS):
        o_ref[0, :, h * SB_HEAD_DIM:(h + 1) * SB_HEAD_DIM] = acc_ref[h].astype(BF16)


def _diag_mask(t):
    row = lax.broadcasted_iota(jnp.int32, (t, t), 0)
    col = lax.broadcasted_iota(jnp.int32, (t, t), 1)
    return col < row


def _attn_prompt_kernel(q_ref, k_ref, v_ref, o_ref, acc_ref, carry_ref, hl_ref, lb_ref, *, tq):
    i = pl.program_id(1)
    tk = tq
    m2 = _suffix_matrix(tk)

    def kv_at(s0):
        return lambda h, sl: (k_ref[0, pl.ds(s0, tk), sl], v_ref[0, pl.ds(s0, tk), sl])

    _sb_heads_batched(q_ref, kv_at(pl.multiple_of(i * tq, tq)), acc_ref, carry_ref, hl_ref, lb_ref,
                      _diag_mask(tq), m2, True)

    def body(st):
        jb, _ = st
        _sb_heads_batched(q_ref, kv_at(pl.multiple_of(jb * tk, tk)), acc_ref, carry_ref, hl_ref, lb_ref,
                          None, m2, False)
        return jb - 1, _all_underflowed(carry_ref)

    lax.while_loop(lambda st: (st[0] >= 0) & (st[1] == 0), body, (i - 1, jnp.int32(0)))
    _store_heads(o_ref, acc_ref)


def _attn_prompt_call(q, kb, vb, *, tq):
    b, n, w = q.shape
    qspec = pl.BlockSpec((1, tq, w), lambda bi, i: (bi, i, 0))
    kvspec = pl.BlockSpec((1, n, w), lambda bi, i: (bi, 0, 0))
    return pl.pallas_call(
        functools.partial(_attn_prompt_kernel, tq=tq),
        grid=(b, n // tq),
        in_specs=[qspec, kvspec, kvspec],
        out_specs=qspec,
        out_shape=jax.ShapeDtypeStruct((b, n, w), BF16),
        scratch_shapes=[pltpu.VMEM((SB_HEADS, tq, SB_HEAD_DIM), F32),
                        pltpu.VMEM((SB_HEADS, tq, 128), F32),
                        pltpu.VMEM((SB_HEADS * tq, 2 * tq), BF16),
                        pltpu.VMEM((SB_HEADS, tq, tq), F32)],
        compiler_params=pltpu.CompilerParams(
            dimension_semantics=("arbitrary", "arbitrary"), vmem_limit_bytes=VMEM_LIMIT),
        name="attn_prompt",
    )(q, kb, vb)


def _attn_sample_kernel(q_ref, k_ref, v_ref, ck_hbm, cv_hbm, o_ref, acc_ref, carry_ref, hl_ref, lb_ref,
                        kbuf, vbuf, sem, *, layer, tq, tk, nblk):
    b = pl.program_id(0)

    def copies(jb, slot):
        rows = pl.ds(jb * tk, tk)
        return (pltpu.make_async_copy(ck_hbm.at[layer, b, rows], kbuf.at[slot], sem.at[0, slot]),
                pltpu.make_async_copy(cv_hbm.at[layer, b, rows], vbuf.at[slot], sem.at[1, slot]))

    for c in copies(nblk - 1, 0):
        c.start()

    _sb_heads(q_ref, lambda h, sl: (k_ref[0, :, sl], v_ref[0, :, sl]), acc_ref, carry_ref,
              _diag_mask(tq), _suffix_matrix(tq), True)
    m2 = _suffix_matrix(tk)

    def body(st):
        jb, _ = st
        slot = (nblk - 1 - jb) & 1
        for c in copies(jb, slot):
            c.wait()

        @pl.when(jb > 0)
        def _():
            for c in copies(jb - 1, 1 - slot):
                c.start()

        _sb_heads_batched(
            q_ref, lambda h, sl: (kbuf[slot, :, h, :].astype(BF16), vbuf[slot, :, h, :].astype(BF16)),
            acc_ref, carry_ref, hl_ref, lb_ref, None, m2, False)
        return jb - 1, _all_underflowed(carry_ref)

    jb_end, _ = lax.while_loop(lambda st: (st[0] >= 0) & (st[1] == 0), body,
                               (jnp.int32(nblk - 1), jnp.int32(0)))

    @pl.when(jb_end >= 0)
    def _():
        for c in copies(jb_end, (nblk - 1 - jb_end) & 1):
            c.wait()

    _store_heads(o_ref, acc_ref)


def _attn_sample_call(q, kb, vb, cache_k, cache_v, layer, *, tk):
    b, n, w = q.shape
    past = cache_k.shape[2]
    tk = min(tk, past)
    assert past % tk == 0 and past >= tk
    newspec = pl.BlockSpec((1, n, w), lambda bi: (bi, 0, 0))
    anyspec = pl.BlockSpec(memory_space=pl.ANY)
    return pl.pallas_call(
        functools.partial(_attn_sample_kernel, layer=layer, tq=n, tk=tk, nblk=past // tk),
        grid=(b,),
        in_specs=[newspec, newspec, newspec, anyspec, anyspec],
        out_specs=newspec,
        out_shape=jax.ShapeDtypeStruct((b, n, w), BF16),
        scratch_shapes=[pltpu.VMEM((SB_HEADS, n, SB_HEAD_DIM), F32),
                        pltpu.VMEM((SB_HEADS, n, 128), F32),
                        pltpu.VMEM((SB_HEADS * n, 2 * tk), BF16),
                        pltpu.VMEM((SB_HEADS, n, tk), F32),
                        pltpu.VMEM((2, tk, SB_HEADS, SB_HEAD_DIM), F32),
                        pltpu.VMEM((2, tk, SB_HEADS, SB_HEAD_DIM), F32),
                        pltpu.SemaphoreType.DMA((2, 2))],
        compiler_params=pltpu.CompilerParams(
            dimension_semantics=("arbitrary",), vmem_limit_bytes=VMEM_LIMIT),
        name="attn_sample",
    )(q, kb, vb, cache_k, cache_v)


def _outproj_kernel(x_ref, po_ref, at_ref, mod_ref, w_out_ref, g_ref, b_ref, x1_ref, mix_ref, *, bt, tn):
    for s in range(bt):
        rows = slice(s * tn, (s + 1) * tn)
        mix_ref[rows, :] = (_bdot(po_ref[s], w_out_ref[0:POOL_WIDTH, :])
                            + _bdot(at_ref[s], w_out_ref[POOL_WIDTH:, :]))
    for s in range(bt):
        rows = slice(s * tn, (s + 1) * tn)
        g1 = mod_ref[s, 2:3, :]
        y = ALPHA * x_ref[s] + (1.0 + g1) * mix_ref[rows, :]
        x1_ref[s] = _norm(y) * g_ref[...] + b_ref[...]


def _outproj_call(x, po, at, mod, w_out, ln_g, ln_b, *, bt, tn):
    b, n, d = x.shape
    tok = lambda width: pl.BlockSpec((bt, tn, width), lambda bi, j: (bi, j, 0))
    const2 = lambda s: pl.BlockSpec(s, lambda bi, j: (0, 0))
    return pl.pallas_call(
        functools.partial(_outproj_kernel, bt=bt, tn=tn),
        grid=(b // bt, n // tn),
        in_specs=[tok(d), tok(POOL_WIDTH), tok(SB_WIDTH),
                  pl.BlockSpec((bt, 6, d), lambda bi, j: (bi, 0, 0)),
                  const2(w_out.shape), const2((1, d)), const2((1, d))],
        out_specs=tok(d),
        out_shape=jax.ShapeDtypeStruct((b, n, d), F32),
        scratch_shapes=[pltpu.VMEM((bt * tn, d), F32)],
        compiler_params=pltpu.CompilerParams(
            dimension_semantics=("arbitrary", "arbitrary"), vmem_limit_bytes=VMEM_LIMIT),
        name="outproj",
    )(x, po, at, mod, w_out, ln_g, ln_b)


def _route(logits):
    col = lax.broadcasted_iota(jnp.int32, logits.shape, 1)
    colf = col.astype(F32)
    is_group = col < N_GROUPS
    gl = jnp.where(is_group, logits, NEG_BIG)
    gmax = jnp.max(gl, axis=1, keepdims=True)
    gsum = jnp.sum(jnp.where(is_group, jnp.exp(gl - gmax), 0.0), axis=1, keepdims=True)
    pg = 1.0 / gsum
    gidx = jnp.min(jnp.where(gl == gmax, colf, 1e9), axis=1, keepdims=True)
    egrp = ((col - N_GROUPS) >> 2).astype(F32)
    in_group = (col >= N_GROUPS) & (col < N_GROUPS + N_EXPERTS) & (egrp == gidx)
    el = jnp.where(in_group, logits, NEG_BIG)
    m1 = jnp.max(el, axis=1, keepdims=True)
    i1 = jnp.min(jnp.where(el == m1, colf, 1e9), axis=1, keepdims=True)
    el2 = jnp.where(colf == i1, NEG_BIG, el)
    m2 = jnp.max(el2, axis=1, keepdims=True)
    i2 = jnp.min(jnp.where(el2 == m2, colf, 1e9), axis=1, keepdims=True)
    r = jnp.exp(m2 - m1)
    w1 = pg / (1.0 + r)
    w2 = pg * r / (1.0 + r)
    return jnp.where(colf == i1, w1, 0.0) + jnp.where(colf == i2, w2, 0.0)


def _moe_kernel(x1_ref, mod_ref, wr_ref, br_ref, wup_ref, wdn_ref, g_ref, b_ref, o_ref,
                hn_ref, gate_ref, acc_ref, *, bt, tn):
    e = pl.program_id(2)

    @pl.when(e == 0)
    def _():
        for s in range(bt):
            sh2 = mod_ref[s, 3:4, :]
            sc2 = mod_ref[s, 4:5, :]
            hn_ref[s * tn:(s + 1) * tn, :] = (_norm(x1_ref[s]) * (1.0 + sc2) + sh2).astype(BF16)
        logits = _bdot(hn_ref[...], wr_ref[...]) + br_ref[...]
        gate_ref[...] = _route(logits)
        acc_ref[...] = jnp.zeros_like(acc_ref)

    hid = _bdot(hn_ref[...], wup_ref[0])
    a = hid[:, 0:D_EXPERT]
    u = hid[:, D_EXPERT:]
    gate = gate_ref[...]
    col = lax.broadcasted_iota(jnp.int32, gate.shape, 1)
    gcol = jnp.sum(jnp.where(col == e + N_GROUPS, gate, 0.0), axis=1, keepdims=True)
    act = a * (1.0 / (1.0 + jnp.exp(-a))) * u * gcol
    acc_ref[...] += _bdot(act.astype(BF16), wdn_ref[0])

    @pl.when(e == pl.num_programs(2) - 1)
    def _():
        for s in range(bt):
            g2 = mod_ref[s, 5:6, :]
            y = ALPHA * x1_ref[s] + (1.0 + g2) * acc_ref[s * tn:(s + 1) * tn, :]
            o_ref[s] = _norm(y) * g_ref[...] + b_ref[...]


def _moe_call(x1, mod, w_r, b_r, w_up, w_down, ln_g, ln_b, *, bt, tn):
    b, n, d = x1.shape
    ne = w_up.shape[0]
    tm = bt * tn
    tok = pl.BlockSpec((bt, tn, d), lambda bi, j, e: (bi, j, 0))
    const2 = lambda s: pl.BlockSpec(s, lambda bi, j, e: (0, 0))
    return pl.pallas_call(
        functools.partial(_moe_kernel, bt=bt, tn=tn),
        grid=(b // bt, n // tn, ne),
        in_specs=[tok,
                  pl.BlockSpec((bt, 6, d), lambda bi, j, e: (bi, 0, 0)),
                  const2(w_r.shape), const2((1, ROUTER_LANES)),
                  pl.BlockSpec((1,) + w_up.shape[1:], lambda bi, j, e: (e, 0, 0)),
                  pl.BlockSpec((1,) + w_down.shape[1:], lambda bi, j, e: (e, 0, 0)),
                  const2((1, d)), const2((1, d))],
        out_specs=tok,
        out_shape=jax.ShapeDtypeStruct((b, n, d), F32),
        scratch_shapes=[pltpu.VMEM((tm, d), BF16),
                        pltpu.VMEM((tm, ROUTER_LANES), F32),
                        pltpu.VMEM((tm, d), F32)],
        compiler_params=pltpu.CompilerParams(
            dimension_semantics=("arbitrary", "arbitrary", "arbitrary"), vmem_limit_bytes=VMEM_LIMIT),
        name="moe",
    )(x1, mod, w_r, b_r, w_up, w_down, ln_g, ln_b)


def _pick(n, pref):
    t = min(n, pref)
    while n % t:
        t //= 2
    return t


def kernel(x_prompt, x_sample, c_prompt, c_sample, cache_k, cache_v, state_pool, w_mod, b_mod, w_in, w_pool, pool_scale, w_out, ln1_g, ln1_b, ln2_g, ln2_b, w_group, b_group, w_router, b_router, w_up, w_down):
    depth = w_mod.shape[0]
    bp, seq, d = x_prompt.shape
    bs, dseq, _ = x_sample.shape
    past = cache_k.shape[2]

    w_mod_b = w_mod.astype(BF16)
    w_in_b = w_in.astype(BF16)
    w_pool_b = w_pool.astype(BF16)
    w_out_b = w_out.astype(BF16)
    w_up_b = w_up.astype(BF16)
    w_down_b = w_down.astype(BF16)
    pad = jnp.zeros((depth, d, ROUTER_LANES - N_GROUPS - N_EXPERTS), F32)
    w_r = jnp.concatenate([w_group, w_router, pad], axis=-1).astype(BF16)
    b_r = jnp.concatenate([b_group, b_router, pad[:, 0, :]], axis=-1).reshape(depth, 1, ROUTER_LANES)

    mod_all = _mod_call(jnp.concatenate([c_prompt, c_sample], axis=0), w_mod_b, b_mod)
    mod_p = mod_all[:, :bp].reshape(depth, bp, 6, d)
    mod_s = mod_all[:, bp:].reshape(depth, bs, 6, d)

    tm_p = _pick(seq, 512)
    tq_p = _pick(seq, 256)
    tn_p = _pick(seq, 1024)
    bt_s = _pick(bs, max(1, 1024 // dseq))

    def run_layer(l, x, mod, hist16, pos0, is_prompt):
        q, k, v, kb, vb, po, new_pool = _inproj_call(
            x, mod, hist16, w_in_b[l], w_pool_b[l], pool_scale[l].reshape(1, POOL_WIDTH),
            tm=tm_p if is_prompt else dseq, pos0=pos0)
        if is_prompt:
            at = _attn_prompt_call(q, kb, vb, tq=tq_p)
            bt, tn = 1, tn_p
        else:
            at = _attn_sample_call(q, kb, vb, cache_k, cache_v, l, tk=256)
            bt, tn = bt_s, dseq
        x1 = _outproj_call(x, po, at, mod, w_out_b[l], ln1_g[l].reshape(1, d), ln1_b[l].reshape(1, d),
                           bt=bt, tn=tn)
        x2 = _moe_call(x1, mod, w_r[l], b_r[l], w_up_b[l], w_down_b[l],
                       ln2_g[l].reshape(1, d), ln2_b[l].reshape(1, d), bt=bt, tn=tn)
        b, n = x.shape[0], x.shape[1]
        return (x2, k.reshape(b, n, SB_HEADS, SB_HEAD_DIM), v.reshape(b, n, SB_HEADS, SB_HEAD_DIM),
                new_pool[:, HIST_ROWS - POOL_HIST:])

    y = x_prompt
    kp, vp, pp = [], [], []
    zero_hist = jnp.zeros((bp, HIST_ROWS, POOL_WIDTH), F32)
    for l in range(depth):
        y, k_new, v_new, p_new = run_layer(l, y, mod_p[l], zero_hist, 0, True)
        kp.append(k_new); vp.append(v_new); pp.append(p_new)
    y_prompt = y

    y = x_sample
    kd, vd, pd = [], [], []
    hist_s = jnp.pad(state_pool, ((0, 0), (0, 0), (HIST_ROWS - POOL_HIST, 0), (0, 0)))
    for l in range(depth):
        y, k_new, v_new, p_new = run_layer(l, y, mod_s[l], hist_s[l], past, False)
        kd.append(k_new); vd.append(v_new); pd.append(p_new)
    y_sample = y

    return (y_prompt, y_sample, jnp.stack(kp), jnp.stack(vp), jnp.stack(pp),
            jnp.stack(kd), jnp.stack(vd), jnp.stack(pd))
```

```python
import functools

import jax
import jax.numpy as jnp
from jax import lax
from jax.experimental import pallas as pl
from jax.experimental.pallas import tpu as pltpu

F32 = jnp.float32
BF16 = jnp.bfloat16

D_MODEL = 1024
POOL_WINDOWS = (2, 4, 8, 16)
POOL_WIDTH = 512
POOL_GROUP_DIM = 128
POOL_HIST = 15
HIST_ROWS = 16
SB_WIDTH = 512
SB_HEADS = 8
SB_HEAD_DIM = 64
N_GROUPS = 4
EXPERTS_PER_GROUP = 4
N_EXPERTS = 16
D_EXPERT = 256
ROUTER_LANES = 128
DEPTH = 2
ALPHA = (2 * DEPTH) ** 0.25
LN_EPS = 1e-5
QK_SCALE = 1.0 / 8.0
LOG_UNDERFLOW = -104.0
NEG_BIG = -1e30
VMEM_LIMIT = 56 * 1024 * 1024


def _norm(x):
    mu = jnp.mean(x, axis=-1, keepdims=True)
    xc = x - mu
    var = jnp.mean(xc * xc, axis=-1, keepdims=True)
    return xc * lax.rsqrt(var + LN_EPS)


def _bdot(a, b):
    return jnp.dot(a, b, preferred_element_type=F32)


def _mod_kernel(c_ref, w_ref, b_ref, o_ref):
    o_ref[0] = _bdot(c_ref[...].astype(BF16), w_ref[0]) + b_ref[0]


def _mod_call(c_all, w_mod, b_mod):
    depth, d, n6 = w_mod.shape
    bc = c_all.shape[0]
    tn = 1536
    return pl.pallas_call(
        _mod_kernel,
        grid=(depth, n6 // tn),
        in_specs=[
            pl.BlockSpec((bc, d), lambda l, j: (0, 0)),
            pl.BlockSpec((1, d, tn), lambda l, j: (l, 0, j)),
            pl.BlockSpec((1, 1, tn), lambda l, j: (l, 0, j)),
        ],
        out_specs=pl.BlockSpec((1, bc, tn), lambda l, j: (l, 0, j)),
        out_shape=jax.ShapeDtypeStruct((depth, bc, n6), F32),
        name="mod",
    )(c_all, w_mod, b_mod.reshape(depth, 1, n6))


def _inproj_kernel(x_ref, mod_ref, hist_ref, w_in_ref, w_pool_ref, ps_ref,
                   q_ref, k_ref, v_ref, kb_ref, vb_ref, po_ref, np_ref, ue_ref, *, tm, pos0):
    j = pl.program_id(1)
    x = x_ref[0]
    sh1 = mod_ref[0, 0:1, :]
    sc1 = mod_ref[0, 1:2, :]
    hn = (_norm(x) * (1.0 + sc1) + sh1).astype(BF16)
    proj = _bdot(hn, w_in_ref[...])
    u = proj[:, 0:POOL_WIDTH]
    q = proj[:, POOL_WIDTH:POOL_WIDTH + SB_WIDTH]
    k = proj[:, POOL_WIDTH + SB_WIDTH:POOL_WIDTH + 2 * SB_WIDTH]
    v = proj[:, POOL_WIDTH + 2 * SB_WIDTH:]
    q_ref[0] = (q * QK_SCALE).astype(BF16)
    k_ref[0] = k
    v_ref[0] = v
    kb_ref[0] = k.astype(BF16)
    vb_ref[0] = v.astype(BF16)

    @pl.when(j == 0)
    def _():
        ue_ref[0:HIST_ROWS, :] = hist_ref[0]

    ue_ref[HIST_ROWS:HIST_ROWS + tm, :] = u
    pos = pos0 + j * tm + lax.broadcasted_iota(jnp.int32, (tm, 1), 0)
    outs = []
    for g, w in enumerate(POOL_WINDOWS):
        c0 = g * POOL_GROUP_DIM
        c1 = c0 + POOL_GROUP_DIM
        acc = ue_ref[HIST_ROWS:HIST_ROWS + tm, c0:c1]
        for dlt in range(1, w):
            acc = acc + ue_ref[HIST_ROWS - dlt:HIST_ROWS - dlt + tm, c0:c1]
        cnt = jnp.minimum(w, pos + 1).astype(F32)
        pooled = acc / cnt - u[:, c0:c1]
        mixed = _bdot(pooled.astype(BF16), w_pool_ref[g])
        outs.append(mixed * ps_ref[:, c0:c1])
    po_ref[0] = jnp.concatenate(outs, axis=1).astype(BF16)

    tail = ue_ref[tm:tm + HIST_ROWS, :]
    ue_ref[0:HIST_ROWS, :] = tail

    @pl.when(j == pl.num_programs(1) - 1)
    def _():
        np_ref[0] = tail


def _inproj_call(x, mod, hist16, w_in, w_pool, pool_scale, *, tm, pos0):
    b, n, d = x.shape
    nproj = w_in.shape[1]
    tok = lambda width: pl.BlockSpec((1, tm, width), lambda bi, j: (bi, j, 0))
    per_b = lambda rows, width: pl.BlockSpec((1, rows, width), lambda bi, j: (bi, 0, 0))
    const2 = lambda s: pl.BlockSpec(s, lambda bi, j: (0, 0))
    return pl.pallas_call(
        functools.partial(_inproj_kernel, tm=tm, pos0=pos0),
        grid=(b, n // tm),
        in_specs=[
            tok(d),
            per_b(6, d),
            per_b(HIST_ROWS, POOL_WIDTH),
            const2((d, nproj)),
            pl.BlockSpec(w_pool.shape, lambda bi, j: (0, 0, 0)),
            const2((1, POOL_WIDTH)),
        ],
        out_specs=[tok(SB_WIDTH), tok(SB_WIDTH), tok(SB_WIDTH), tok(SB_WIDTH), tok(SB_WIDTH),
                   tok(POOL_WIDTH), per_b(HIST_ROWS, POOL_WIDTH)],
        out_shape=[
            jax.ShapeDtypeStruct((b, n, SB_WIDTH), BF16),
            jax.ShapeDtypeStruct((b, n, SB_WIDTH), F32),
            jax.ShapeDtypeStruct((b, n, SB_WIDTH), F32),
            jax.ShapeDtypeStruct((b, n, SB_WIDTH), BF16),
            jax.ShapeDtypeStruct((b, n, SB_WIDTH), BF16),
            jax.ShapeDtypeStruct((b, n, POOL_WIDTH), BF16),
            jax.ShapeDtypeStruct((b, HIST_ROWS, POOL_WIDTH), F32),
        ],
        scratch_shapes=[pltpu.VMEM((tm + HIST_ROWS, POOL_WIDTH), F32)],
        compiler_params=pltpu.CompilerParams(
            dimension_semantics=("arbitrary", "arbitrary"), vmem_limit_bytes=VMEM_LIMIT),
        name="inproj",
    )(x, mod, hist16, w_in, w_pool, pool_scale)


def _suffix_matrix(tk):
    r = lax.broadcasted_iota(jnp.int32, (2 * tk, tk), 0)
    c = lax.broadcasted_iota(jnp.int32, (2 * tk, tk), 1)
    r = jnp.where(r >= tk, r - tk, r)
    return jnp.where(r > c, 1.0, 0.0).astype(BF16)


def _sb_block(qh, kblk, vblk, carry, mask, m2):
    tk = kblk.shape[0]
    z = lax.dot_general(qh, kblk, (((1,), (1,)), ((), ())), preferred_element_type=F32)
    l1p = jnp.log(1.0 + jnp.exp(-jnp.abs(z)))
    log_beta = jnp.minimum(z, 0.0) - l1p
    log_1m = log_beta - z
    if mask is not None:
        log_1m = jnp.where(mask, log_1m, 0.0)
    hi = log_1m.astype(BF16)
    lo = (log_1m - hi.astype(F32)).astype(BF16)
    if tk % 128 == 0:
        suffix = _bdot(jnp.concatenate([hi, lo], axis=1), m2)
    else:
        suffix = _bdot(hi, m2[0:tk]) + _bdot(lo, m2[0:tk])
    arg = log_beta + suffix
    if carry is not None:
        arg = arg + (jnp.concatenate([carry] * (tk // 128), axis=1) if tk > 128 else carry)
    a = jnp.exp(arg)
    if mask is not None:
        a = jnp.where(mask, a, 0.0)
    return _bdot(a.astype(BF16), vblk), jnp.sum(log_1m, axis=1, keepdims=True)


def _sb_heads(q_ref, get_kv, acc_ref, carry_ref, mask, m2, first):
    tq = acc_ref.shape[1]
    for h in range(SB_HEADS):
        sl = slice(h * SB_HEAD_DIM, (h + 1) * SB_HEAD_DIM)
        kblk, vblk = get_kv(h, sl)
        pv, tot = _sb_block(q_ref[0, :, sl], kblk, vblk, None if first else carry_ref[h], mask, m2)
        tot = jnp.broadcast_to(tot, (tq, 128))
        if first:
            acc_ref[h] = pv
            carry_ref[h] = tot
        else:
            acc_ref[h] += pv
            carry_ref[h] += tot


def _sb_heads_batched(q_ref, get_k, get_v, acc_ref, carry_ref, hl_ref, lb_ref, mask, m2, first,
                      keys_minor=False):
    tq = acc_ref.shape[1]
    tk = m2.shape[1]
    k_contract = 0 if keys_minor else 1
    tots = []
    for h in range(SB_HEADS):
        sl = slice(h * SB_HEAD_DIM, (h + 1) * SB_HEAD_DIM)
        z = lax.dot_general(q_ref[0, :, sl], get_k(h, sl), (((1,), (k_contract,)), ((), ())),
                            preferred_element_type=F32)
        l1p = jnp.log(1.0 + jnp.exp(-jnp.abs(z)))
        log_beta = jnp.minimum(z, 0.0) - l1p
        log_1m = log_beta - z
        if mask is not None:
            log_1m = jnp.where(mask, log_1m, 0.0)
        hi = log_1m.astype(BF16)
        lo = (log_1m - hi.astype(F32)).astype(BF16)
        hl_ref[h * tq:(h + 1) * tq, :] = jnp.concatenate([hi, lo], axis=1)
        lb_ref[h] = log_beta
        tots.append(jnp.broadcast_to(jnp.sum(log_1m, axis=1, keepdims=True), (tq, 128)))
    suffix = _bdot(hl_ref[...], m2)
    for h in range(SB_HEADS):
        sl = slice(h * SB_HEAD_DIM, (h + 1) * SB_HEAD_DIM)
        arg = lb_ref[h] + suffix[h * tq:(h + 1) * tq]
        if not first:
            arg = arg + jnp.concatenate([carry_ref[h]] * (tk // 128), axis=1)
        a = jnp.exp(arg)
        if mask is not None:
            a = jnp.where(mask, a, 0.0)
        pv = lax.dot_general(a.astype(BF16), get_v(h, sl), (((1,), (1 - k_contract,)), ((), ())),
                             preferred_element_type=F32)
        if first:
            acc_ref[h] = pv
            carry_ref[h] = tots[h]
        else:
            acc_ref[h] += pv
            carry_ref[h] += tots[h]


def _all_underflowed(carry_ref):
    return (jnp.max(carry_ref[...]) < LOG_UNDERFLOW).astype(jnp.int32)


def _store_heads(o_ref, acc_ref):
    for h in range(SB_HEADS):
        o_ref[0, :, h * SB_HEAD_DIM:(h + 1) * SB_HEAD_DIM] = acc_ref[h].astype(BF16)


def _diag_mask(t):
    row = lax.broadcasted_iota(jnp.int32, (t, t), 0)
    col = lax.broadcasted_iota(jnp.int32, (t, t), 1)
    return col < row


def _attn_prompt_kernel(q_ref, k_ref, v_ref, o_ref, acc_ref, carry_ref, hl_ref, lb_ref, *, tq):
    i = pl.program_id(1)
    tk = tq
    m2 = _suffix_matrix(tk)

    def block(s0, mask, first):
        _sb_heads_batched(q_ref, lambda h, sl: k_ref[0, pl.ds(s0, tk), sl],
                          lambda h, sl: v_ref[0, pl.ds(s0, tk), sl],
                          acc_ref, carry_ref, hl_ref, lb_ref, mask, m2, first)

    block(pl.multiple_of(i * tq, tq), _diag_mask(tq), True)

    def body(st):
        jb, _ = st
        block(pl.multiple_of(jb * tk, tk), None, False)
        return jb - 1, _all_underflowed(carry_ref)

    lax.while_loop(lambda st: (st[0] >= 0) & (st[1] == 0), body, (i - 1, jnp.int32(0)))
    _store_heads(o_ref, acc_ref)


def _attn_prompt_call(q, kb, vb, *, tq):
    b, n, w = q.shape
    qspec = pl.BlockSpec((1, tq, w), lambda bi, i: (bi, i, 0))
    kvspec = pl.BlockSpec((1, n, w), lambda bi, i: (bi, 0, 0))
    return pl.pallas_call(
        functools.partial(_attn_prompt_kernel, tq=tq),
        grid=(b, n // tq),
        in_specs=[qspec, kvspec, kvspec],
        out_specs=qspec,
        out_shape=jax.ShapeDtypeStruct((b, n, w), BF16),
        scratch_shapes=[pltpu.VMEM((SB_HEADS, tq, SB_HEAD_DIM), F32),
                        pltpu.VMEM((SB_HEADS, tq, 128), F32),
                        pltpu.VMEM((SB_HEADS * tq, 2 * tq), BF16),
                        pltpu.VMEM((SB_HEADS, tq, tq), F32)],
        compiler_params=pltpu.CompilerParams(
            dimension_semantics=("arbitrary", "arbitrary"), vmem_limit_bytes=VMEM_LIMIT),
        name="attn_prompt",
    )(q, kb, vb)


def _attn_sample_kernel(q_ref, k_ref, v_ref, ck_hbm, cv_hbm, o_ref, acc_ref, carry_ref, hl_ref, lb_ref,
                        kbuf, vbuf, sem, *, layer, tq, tk, nblk):
    b = pl.program_id(0)

    def copies(jb, slot):
        cols = pl.ds(jb * tk, tk)
        return (pltpu.make_async_copy(ck_hbm.at[layer, b, :, :, cols], kbuf.at[slot], sem.at[0, slot]),
                pltpu.make_async_copy(cv_hbm.at[layer, b, :, :, cols], vbuf.at[slot], sem.at[1, slot]))

    for c in copies(nblk - 1, 0):
        c.start()

    _sb_heads(q_ref, lambda h, sl: (k_ref[0, :, sl], v_ref[0, :, sl]), acc_ref, carry_ref,
              _diag_mask(tq), _suffix_matrix(tq), True)
    m2 = _suffix_matrix(tk)

    def body(st):
        jb, _ = st
        slot = (nblk - 1 - jb) & 1
        for c in copies(jb, slot):
            c.wait()

        @pl.when(jb > 0)
        def _():
            for c in copies(jb - 1, 1 - slot):
                c.start()

        _sb_heads_batched(q_ref, lambda h, sl: kbuf[slot, h].astype(BF16),
                          lambda h, sl: vbuf[slot, h].astype(BF16),
                          acc_ref, carry_ref, hl_ref, lb_ref, None, m2, False, keys_minor=True)
        return jb - 1, _all_underflowed(carry_ref)

    jb_end, _ = lax.while_loop(lambda st: (st[0] >= 0) & (st[1] == 0), body,
                               (jnp.int32(nblk - 1), jnp.int32(0)))

    @pl.when(jb_end >= 0)
    def _():
        for c in copies(jb_end, (nblk - 1 - jb_end) & 1):
            c.wait()

    _store_heads(o_ref, acc_ref)


def _attn_sample_call(q, kb, vb, cache_kt, cache_vt, layer, *, tk):
    b, n, w = q.shape
    past = cache_kt.shape[4]
    tk = min(tk, past)
    assert past % tk == 0 and past >= tk
    newspec = pl.BlockSpec((1, n, w), lambda bi: (bi, 0, 0))
    anyspec = pl.BlockSpec(memory_space=pl.ANY)
    return pl.pallas_call(
        functools.partial(_attn_sample_kernel, layer=layer, tq=n, tk=tk, nblk=past // tk),
        grid=(b,),
        in_specs=[newspec, newspec, newspec, anyspec, anyspec],
        out_specs=newspec,
        out_shape=jax.ShapeDtypeStruct((b, n, w), BF16),
        scratch_shapes=[pltpu.VMEM((SB_HEADS, n, SB_HEAD_DIM), F32),
                        pltpu.VMEM((SB_HEADS, n, 128), F32),
                        pltpu.VMEM((SB_HEADS * n, 2 * tk), BF16),
                        pltpu.VMEM((SB_HEADS, n, tk), F32),
                        pltpu.VMEM((2, SB_HEADS, SB_HEAD_DIM, tk), F32),
                        pltpu.VMEM((2, SB_HEADS, SB_HEAD_DIM, tk), F32),
                        pltpu.SemaphoreType.DMA((2, 2))],
        compiler_params=pltpu.CompilerParams(
            dimension_semantics=("arbitrary",), vmem_limit_bytes=VMEM_LIMIT),
        name="attn_sample",
    )(q, kb, vb, cache_kt, cache_vt)


def _outproj_kernel(x_ref, po_ref, at_ref, mod_ref, w_out_ref, g_ref, b_ref, x1_ref, mix_ref, *, bt, tn):
    for s in range(bt):
        rows = slice(s * tn, (s + 1) * tn)
        mix_ref[rows, :] = (_bdot(po_ref[s], w_out_ref[0:POOL_WIDTH, :])
                            + _bdot(at_ref[s], w_out_ref[POOL_WIDTH:, :]))
    for s in range(bt):
        rows = slice(s * tn, (s + 1) * tn)
        g1 = mod_ref[s, 2:3, :]
        y = ALPHA * x_ref[s] + (1.0 + g1) * mix_ref[rows, :]
        x1_ref[s] = _norm(y) * g_ref[...] + b_ref[...]


def _outproj_call(x, po, at, mod, w_out, ln_g, ln_b, *, bt, tn):
    b, n, d = x.shape
    tok = lambda width: pl.BlockSpec((bt, tn, width), lambda bi, j: (bi, j, 0))
    const2 = lambda s: pl.BlockSpec(s, lambda bi, j: (0, 0))
    return pl.pallas_call(
        functools.partial(_outproj_kernel, bt=bt, tn=tn),
        grid=(b // bt, n // tn),
        in_specs=[tok(d), tok(POOL_WIDTH), tok(SB_WIDTH),
                  pl.BlockSpec((bt, 6, d), lambda bi, j: (bi, 0, 0)),
                  const2(w_out.shape), const2((1, d)), const2((1, d))],
        out_specs=tok(d),
        out_shape=jax.ShapeDtypeStruct((b, n, d), F32),
        scratch_shapes=[pltpu.VMEM((bt * tn, d), F32)],
        compiler_params=pltpu.CompilerParams(
            dimension_semantics=("arbitrary", "arbitrary"), vmem_limit_bytes=VMEM_LIMIT),
        name="outproj",
    )(x, po, at, mod, w_out, ln_g, ln_b)


def _route(logits):
    col = lax.broadcasted_iota(jnp.int32, logits.shape, 1)
    colf = col.astype(F32)
    is_group = col < N_GROUPS
    gl = jnp.where(is_group, logits, NEG_BIG)
    gmax = jnp.max(gl, axis=1, keepdims=True)
    gsum = jnp.sum(jnp.where(is_group, jnp.exp(gl - gmax), 0.0), axis=1, keepdims=True)
    pg = 1.0 / gsum
    gidx = jnp.min(jnp.where(gl == gmax, colf, 1e9), axis=1, keepdims=True)
    egrp = ((col - N_GROUPS) >> 2).astype(F32)
    in_group = (col >= N_GROUPS) & (col < N_GROUPS + N_EXPERTS) & (egrp == gidx)
    el = jnp.where(in_group, logits, NEG_BIG)
    m1 = jnp.max(el, axis=1, keepdims=True)
    i1 = jnp.min(jnp.where(el == m1, colf, 1e9), axis=1, keepdims=True)
    el2 = jnp.where(colf == i1, NEG_BIG, el)
    m2 = jnp.max(el2, axis=1, keepdims=True)
    i2 = jnp.min(jnp.where(el2 == m2, colf, 1e9), axis=1, keepdims=True)
    r = jnp.exp(m2 - m1)
    w1 = pg / (1.0 + r)
    w2 = pg * r / (1.0 + r)
    return jnp.where(colf == i1, w1, 0.0) + jnp.where(colf == i2, w2, 0.0)


def _moe_kernel(x1_ref, mod_ref, wr_ref, br_ref, wup_ref, wdn_ref, g_ref, b_ref, o_ref,
                hn_ref, gate_ref, acc_ref, *, bt, tn):
    g = pl.program_id(2)

    @pl.when(g == 0)
    def _():
        for s in range(bt):
            sh2 = mod_ref[s, 3:4, :]
            sc2 = mod_ref[s, 4:5, :]
            hn_ref[s * tn:(s + 1) * tn, :] = (_norm(x1_ref[s]) * (1.0 + sc2) + sh2).astype(BF16)
        logits = _bdot(hn_ref[...], wr_ref[...]) + br_ref[...]
        gate_ref[...] = _route(logits)

    hn = hn_ref[...]
    gate = gate_ref[...]
    col = lax.broadcasted_iota(jnp.int32, gate.shape, 1)
    y = None
    for e in range(EXPERTS_PER_GROUP):
        hid = _bdot(hn, wup_ref[e])
        a = hid[:, 0:D_EXPERT]
        u = hid[:, D_EXPERT:]
        lane = N_GROUPS + g * EXPERTS_PER_GROUP + e
        gcol = jnp.sum(jnp.where(col == lane, gate, 0.0), axis=1, keepdims=True)
        act = a * (1.0 / (1.0 + jnp.exp(-a))) * u * gcol
        part = _bdot(act.astype(BF16), wdn_ref[e])
        y = part if y is None else y + part

    @pl.when(g == 0)
    def _():
        acc_ref[...] = y

    @pl.when(g > 0)
    def _():
        acc_ref[...] += y

    @pl.when(g == pl.num_programs(2) - 1)
    def _():
        for s in range(bt):
            g2 = mod_ref[s, 5:6, :]
            y2 = ALPHA * x1_ref[s] + (1.0 + g2) * acc_ref[s * tn:(s + 1) * tn, :]
            o_ref[s] = _norm(y2) * g_ref[...] + b_ref[...]


def _moe_call(x1, mod, w_r, b_r, w_up, w_down, ln_g, ln_b, *, bt, tn):
    b, n, d = x1.shape
    tm = bt * tn
    tok = pl.BlockSpec((bt, tn, d), lambda bi, j, g: (bi, j, 0))
    const2 = lambda s: pl.BlockSpec(s, lambda bi, j, g: (0, 0))
    grp = lambda w: pl.BlockSpec((EXPERTS_PER_GROUP,) + w.shape[1:], lambda bi, j, g: (g, 0, 0))
    return pl.pallas_call(
        functools.partial(_moe_kernel, bt=bt, tn=tn),
        grid=(b // bt, n // tn, N_GROUPS),
        in_specs=[tok,
                  pl.BlockSpec((bt, 6, d), lambda bi, j, g: (bi, 0, 0)),
                  const2(w_r.shape), const2((1, ROUTER_LANES)),
                  grp(w_up), grp(w_down),
                  const2((1, d)), const2((1, d))],
        out_specs=tok,
        out_shape=jax.ShapeDtypeStruct((b, n, d), F32),
        scratch_shapes=[pltpu.VMEM((tm, d), BF16),
                        pltpu.VMEM((tm, ROUTER_LANES), F32),
                        pltpu.VMEM((tm, d), F32)],
        compiler_params=pltpu.CompilerParams(
            dimension_semantics=("arbitrary", "arbitrary", "arbitrary"), vmem_limit_bytes=VMEM_LIMIT),
        name="moe",
    )(x1, mod, w_r, b_r, w_up, w_down, ln_g, ln_b)


def _pick(n, pref):
    t = min(n, pref)
    while n % t:
        t //= 2
    return t


def kernel(x_prompt, x_sample, c_prompt, c_sample, cache_k, cache_v, state_pool, w_mod, b_mod, w_in, w_pool, pool_scale, w_out, ln1_g, ln1_b, ln2_g, ln2_b, w_group, b_group, w_router, b_router, w_up, w_down):
    depth = w_mod.shape[0]
    bp, seq, d = x_prompt.shape
    bs, dseq, _ = x_sample.shape
    past = cache_k.shape[2]

    w_mod_b = w_mod.astype(BF16)
    w_in_b = w_in.astype(BF16)
    w_pool_b = w_pool.astype(BF16)
    w_out_b = w_out.astype(BF16)
    w_up_b = w_up.astype(BF16)
    w_down_b = w_down.astype(BF16)
    pad = jnp.zeros((depth, d, ROUTER_LANES - N_GROUPS - N_EXPERTS), F32)
    w_r = jnp.concatenate([w_group, w_router, pad], axis=-1).astype(BF16)
    b_r = jnp.concatenate([b_group, b_router, pad[:, 0, :]], axis=-1).reshape(depth, 1, ROUTER_LANES)

    cache_kt = jnp.transpose(cache_k, (0, 1, 3, 4, 2))
    cache_vt = jnp.transpose(cache_v, (0, 1, 3, 4, 2))

    mod_all = _mod_call(jnp.concatenate([c_prompt, c_sample], axis=0), w_mod_b, b_mod)
    mod_p = mod_all[:, :bp].reshape(depth, bp, 6, d)
    mod_s = mod_all[:, bp:].reshape(depth, bs, 6, d)

    tm_p = _pick(seq, 512)
    tq_p = _pick(seq, 256)
    tn_p = _pick(seq, 1024)
    bt_s = _pick(bs, max(1, 1024 // dseq))

    def run_layer(l, x, mod, hist16, pos0, is_prompt):
        q, k, v, kb, vb, po, new_pool = _inproj_call(
            x, mod, hist16, w_in_b[l], w_pool_b[l], pool_scale[l].reshape(1, POOL_WIDTH),
            tm=tm_p if is_prompt else dseq, pos0=pos0)
        if is_prompt:
            at = _attn_prompt_call(q, kb, vb, tq=tq_p)
            bt, tn = 1, tn_p
        else:
            at = _attn_sample_call(q, kb, vb, cache_kt, cache_vt, l, tk=256)
            bt, tn = bt_s, dseq
        x1 = _outproj_call(x, po, at, mod, w_out_b[l], ln1_g[l].reshape(1, d), ln1_b[l].reshape(1, d),
                           bt=bt, tn=tn)
        x2 = _moe_call(x1, mod, w_r[l], b_r[l], w_up_b[l], w_down_b[l],
                       ln2_g[l].reshape(1, d), ln2_b[l].reshape(1, d), bt=bt, tn=tn)
        b, n = x.shape[0], x.shape[1]
        return (x2, k.reshape(b, n, SB_HEADS, SB_HEAD_DIM), v.reshape(b, n, SB_HEADS, SB_HEAD_DIM),
                new_pool[:, HIST_ROWS - POOL_HIST:])

    y = x_prompt
    kp, vp, pp = [], [], []
    zero_hist = jnp.zeros((bp, HIST_ROWS, POOL_WIDTH), F32)
    for l in range(depth):
        y, k_new, v_new, p_new = run_layer(l, y, mod_p[l], zero_hist, 0, True)
        kp.append(k_new); vp.append(v_new); pp.append(p_new)
    y_prompt = y

    y = x_sample
    kd, vd, pd = [], [], []
    hist_s = jnp.pad(state_pool, ((0, 0), (0, 0), (HIST_ROWS - POOL_HIST, 0), (0, 0)))
    for l in range(depth):
        y, k_new, v_new, p_new = run_layer(l, y, mod_s[l], hist_s[l], past, False)
        kd.append(k_new); vd.append(v_new); pd.append(p_new)
    y_sample = y

    return (y_prompt, y_sample, jnp.stack(kp), jnp.stack(vp), jnp.stack(pp),
            jnp.stack(kd), jnp.stack(vd), jnp.stack(pd))
```

```python
import functools

import jax
import jax.numpy as jnp
from jax import lax
from jax.experimental import pallas as pl
from jax.experimental.pallas import tpu as pltpu

F32 = jnp.float32
BF16 = jnp.bfloat16

D_MODEL = 1024
POOL_WINDOWS = (2, 4, 8, 16)
POOL_WIDTH = 512
POOL_GROUP_DIM = 128
POOL_HIST = 15
HIST_ROWS = 16
SB_WIDTH = 512
SB_HEADS = 8
SB_HEAD_DIM = 64
N_GROUPS = 4
EXPERTS_PER_GROUP = 4
N_EXPERTS = 16
D_EXPERT = 256
ROUTER_LANES = 128
DEPTH = 2
ALPHA = (2 * DEPTH) ** 0.25
LN_EPS = 1e-5
QK_SCALE = 1.0 / 8.0
LOG_UNDERFLOW = -104.0
NEG_BIG = -1e30
VMEM_LIMIT = 56 * 1024 * 1024
ROW_CHUNK = 256


def _norm(x):
    mu = jnp.mean(x, axis=-1, keepdims=True)
    xc = x - mu
    var = jnp.mean(xc * xc, axis=-1, keepdims=True)
    return xc * lax.rsqrt(var + LN_EPS)


def _bdot(a, b):
    return jnp.dot(a, b, preferred_element_type=F32)


def _mod_kernel(c_ref, w_ref, b_ref, o_ref):
    o_ref[0] = _bdot(c_ref[...].astype(BF16), w_ref[0]) + b_ref[0]


def _mod_call(c_all, w_mod, b_mod):
    depth, d, n6 = w_mod.shape
    bc = c_all.shape[0]
    tn = 1536
    return pl.pallas_call(
        _mod_kernel,
        grid=(depth, n6 // tn),
        in_specs=[
            pl.BlockSpec((bc, d), lambda l, j: (0, 0)),
            pl.BlockSpec((1, d, tn), lambda l, j: (l, 0, j)),
            pl.BlockSpec((1, 1, tn), lambda l, j: (l, 0, j)),
        ],
        out_specs=pl.BlockSpec((1, bc, tn), lambda l, j: (l, 0, j)),
        out_shape=jax.ShapeDtypeStruct((depth, bc, n6), F32),
        name="mod",
    )(c_all, w_mod, b_mod.reshape(depth, 1, n6))


def _inproj_kernel(x_ref, mod_ref, hist_ref, w_in_ref, w_pool_ref, ps_ref,
                   q_ref, k_ref, v_ref, kb_ref, vb_ref, po_ref, np_ref, ue_ref, *, tm, rc, pos0):
    j = pl.program_id(1)
    sh1 = mod_ref[0, 0:1, :]
    sc1 = mod_ref[0, 1:2, :]

    @pl.when(j == 0)
    def _():
        ue_ref[0:HIST_ROWS, :] = hist_ref[0]

    for r0 in range(0, tm, rc):
        rows = slice(r0, r0 + rc)
        hn = (_norm(x_ref[0, rows, :]) * (1.0 + sc1) + sh1).astype(BF16)
        proj = _bdot(hn, w_in_ref[...])
        u = proj[:, 0:POOL_WIDTH]
        q = proj[:, POOL_WIDTH:POOL_WIDTH + SB_WIDTH]
        k = proj[:, POOL_WIDTH + SB_WIDTH:POOL_WIDTH + 2 * SB_WIDTH]
        v = proj[:, POOL_WIDTH + 2 * SB_WIDTH:]
        q_ref[0, rows, :] = (q * QK_SCALE).astype(BF16)
        k_ref[0, rows, :] = k
        v_ref[0, rows, :] = v
        kb_ref[0, rows, :] = k.astype(BF16)
        vb_ref[0, rows, :] = v.astype(BF16)
        ue_ref[HIST_ROWS + r0:HIST_ROWS + r0 + rc, :] = u
        pos = pos0 + j * tm + r0 + lax.broadcasted_iota(jnp.int32, (rc, 1), 0)
        outs = []
        for g, w in enumerate(POOL_WINDOWS):
            c0 = g * POOL_GROUP_DIM
            c1 = c0 + POOL_GROUP_DIM
            acc = u[:, c0:c1]
            for dlt in range(1, w):
                acc = acc + ue_ref[HIST_ROWS + r0 - dlt:HIST_ROWS + r0 - dlt + rc, c0:c1]
            cnt = jnp.minimum(w, pos + 1).astype(F32)
            pooled = acc / cnt - u[:, c0:c1]
            mixed = _bdot(pooled.astype(BF16), w_pool_ref[g])
            outs.append(mixed * ps_ref[:, c0:c1])
        po_ref[0, rows, :] = jnp.concatenate(outs, axis=1).astype(BF16)

    tail = ue_ref[tm:tm + HIST_ROWS, :]
    ue_ref[0:HIST_ROWS, :] = tail

    @pl.when(j == pl.num_programs(1) - 1)
    def _():
        np_ref[0] = tail


def _inproj_call(x, mod, hist16, w_in, w_pool, pool_scale, *, tm, pos0):
    b, n, d = x.shape
    nproj = w_in.shape[1]
    tok = lambda width: pl.BlockSpec((1, tm, width), lambda bi, j: (bi, j, 0))
    per_b = lambda rows, width: pl.BlockSpec((1, rows, width), lambda bi, j: (bi, 0, 0))
    const2 = lambda s: pl.BlockSpec(s, lambda bi, j: (0, 0))
    return pl.pallas_call(
        functools.partial(_inproj_kernel, tm=tm, rc=_pick(tm, ROW_CHUNK), pos0=pos0),
        grid=(b, n // tm),
        in_specs=[
            tok(d),
            per_b(6, d),
            per_b(HIST_ROWS, POOL_WIDTH),
            const2((d, nproj)),
            pl.BlockSpec(w_pool.shape, lambda bi, j: (0, 0, 0)),
            const2((1, POOL_WIDTH)),
        ],
        out_specs=[tok(SB_WIDTH), tok(SB_WIDTH), tok(SB_WIDTH), tok(SB_WIDTH), tok(SB_WIDTH),
                   tok(POOL_WIDTH), per_b(HIST_ROWS, POOL_WIDTH)],
        out_shape=[
            jax.ShapeDtypeStruct((b, n, SB_WIDTH), BF16),
            jax.ShapeDtypeStruct((b, n, SB_WIDTH), F32),
            jax.ShapeDtypeStruct((b, n, SB_WIDTH), F32),
            jax.ShapeDtypeStruct((b, n, SB_WIDTH), BF16),
            jax.ShapeDtypeStruct((b, n, SB_WIDTH), BF16),
            jax.ShapeDtypeStruct((b, n, POOL_WIDTH), BF16),
            jax.ShapeDtypeStruct((b, HIST_ROWS, POOL_WIDTH), F32),
        ],
        scratch_shapes=[pltpu.VMEM((tm + HIST_ROWS, POOL_WIDTH), F32)],
        compiler_params=pltpu.CompilerParams(
            dimension_semantics=("arbitrary", "arbitrary"), vmem_limit_bytes=VMEM_LIMIT),
        name="inproj",
    )(x, mod, hist16, w_in, w_pool, pool_scale)


def _suffix_matrix(tk):
    r = lax.broadcasted_iota(jnp.int32, (2 * tk, tk), 0)
    c = lax.broadcasted_iota(jnp.int32, (2 * tk, tk), 1)
    r = jnp.where(r >= tk, r - tk, r)
    return jnp.where(r > c, 1.0, 0.0).astype(BF16)


def _split_hi_lo(x):
    hi = x.astype(BF16)
    return hi, (x - hi.astype(F32)).astype(BF16)


def _sb_block(qh, kblk, vblk, carry, mask, m2):
    tk = kblk.shape[0]
    z = lax.dot_general(qh, kblk, (((1,), (1,)), ((), ())), preferred_element_type=F32)
    l1p = jnp.log(1.0 + jnp.exp(-jnp.abs(z)))
    log_beta = jnp.minimum(z, 0.0) - l1p
    log_1m = log_beta - z
    if mask is not None:
        log_1m = jnp.where(mask, log_1m, 0.0)
    hi, lo = _split_hi_lo(log_1m)
    if tk % 128 == 0:
        suffix = _bdot(jnp.concatenate([hi, lo], axis=1), m2)
    else:
        suffix = _bdot(hi, m2[0:tk]) + _bdot(lo, m2[0:tk])
    arg = log_beta + suffix
    if carry is not None:
        arg = arg + (jnp.concatenate([carry] * (tk // 128), axis=1) if tk > 128 else carry)
    a = jnp.exp(arg)
    if mask is not None:
        a = jnp.where(mask, a, 0.0)
    return _bdot(a.astype(BF16), vblk), jnp.sum(log_1m, axis=1, keepdims=True)


def _sb_heads(q_ref, get_kv, acc_ref, carry_ref, mask, m2, first):
    tq = acc_ref.shape[1]
    for h in range(SB_HEADS):
        sl = slice(h * SB_HEAD_DIM, (h + 1) * SB_HEAD_DIM)
        kblk, vblk = get_kv(h, sl)
        pv, tot = _sb_block(q_ref[0, :, sl], kblk, vblk, None if first else carry_ref[h], mask, m2)
        tot = jnp.broadcast_to(tot, (tq, 128))
        if first:
            acc_ref[h] = pv
            carry_ref[h] = tot
        else:
            acc_ref[h] += pv
            carry_ref[h] += tot


def _sb_chains_batched(chains, acc_ref, carry_ref, hl_ref, lb_ref, mask, m2, first, keys_minor=False):
    tq = acc_ref.shape[1]
    tk = m2.shape[1]
    k_contract = 0 if keys_minor else 1
    tots = []
    for c, (get_q, get_k, _) in enumerate(chains):
        z = lax.dot_general(get_q(), get_k(), (((1,), (k_contract,)), ((), ())),
                            preferred_element_type=F32)
        l1p = jnp.log(1.0 + jnp.exp(-jnp.abs(z)))
        log_beta = jnp.minimum(z, 0.0) - l1p
        log_1m = log_beta - z
        if mask is not None:
            log_1m = jnp.where(mask, log_1m, 0.0)
        hi, lo = _split_hi_lo(log_1m)
        hl_ref[c * tq:(c + 1) * tq, :] = jnp.concatenate([hi, lo], axis=1)
        lb_ref[c] = log_beta
        tots.append(jnp.broadcast_to(jnp.sum(log_1m, axis=1, keepdims=True), (tq, 128)))
    suffix = _bdot(hl_ref[...], m2)
    for c, (_, _, get_v) in enumerate(chains):
        arg = lb_ref[c] + suffix[c * tq:(c + 1) * tq]
        if not first:
            arg = arg + (jnp.concatenate([carry_ref[c]] * (tk // 128), axis=1) if tk > 128 else carry_ref[c])
        a = jnp.exp(arg)
        if mask is not None:
            a = jnp.where(mask, a, 0.0)
        pv = lax.dot_general(a.astype(BF16), get_v(), (((1,), (1 - k_contract,)), ((), ())),
                             preferred_element_type=F32)
        if first:
            acc_ref[c] = pv
            carry_ref[c] = tots[c]
        else:
            acc_ref[c] += pv
            carry_ref[c] += tots[c]


def _head_slices():
    return [slice(h * SB_HEAD_DIM, (h + 1) * SB_HEAD_DIM) for h in range(SB_HEADS)]


def _all_underflowed(carry_ref):
    return (jnp.max(carry_ref[...]) < LOG_UNDERFLOW).astype(jnp.int32)


def _store_heads(o_ref, acc_ref):
    for h in range(SB_HEADS):
        o_ref[0, :, h * SB_HEAD_DIM:(h + 1) * SB_HEAD_DIM] = acc_ref[h].astype(BF16)


def _diag_mask(t):
    row = lax.broadcasted_iota(jnp.int32, (t, t), 0)
    col = lax.broadcasted_iota(jnp.int32, (t, t), 1)
    return col < row


def _attn_prompt_kernel(q_ref, k_ref, v_ref, o_ref, acc_ref, carry_ref, hl_ref, lb_ref, *, tq, ts):
    i = pl.program_id(1)
    nsub = tq // ts
    base = i * nsub
    m2 = _suffix_matrix(ts)
    heads = _head_slices()

    def chains(blocks):
        out = []
        for s in range(nsub):
            k0 = pl.multiple_of(blocks[s] * ts, ts)
            rows = slice(s * ts, (s + 1) * ts)
            for sl in heads:
                out.append((lambda rows=rows, sl=sl: q_ref[0, rows, sl],
                            lambda k0=k0, sl=sl: k_ref[0, pl.ds(k0, ts), sl],
                            lambda k0=k0, sl=sl: v_ref[0, pl.ds(k0, ts), sl]))
        return out

    _sb_chains_batched(chains([base + s for s in range(nsub)]), acc_ref, carry_ref, hl_ref, lb_ref,
                       _diag_mask(ts), m2, True)

    def body(st):
        k, _ = st
        blocks = []
        for s in range(nsub):
            blk = base + s - k
            if s < nsub - 1:
                @pl.when(blk < 0)
                def _(s=s):
                    carry_ref[s * SB_HEADS:(s + 1) * SB_HEADS] = jnp.full((SB_HEADS, ts, 128), NEG_BIG, F32)
            blocks.append(jnp.maximum(blk, 0))
        _sb_chains_batched(chains(blocks), acc_ref, carry_ref, hl_ref, lb_ref, None, m2, False)
        return k + 1, _all_underflowed(carry_ref)

    lax.while_loop(lambda st: (base + nsub - 1 - st[0] >= 0) & (st[1] == 0), body,
                   (jnp.int32(1), jnp.int32(0)))
    for s in range(nsub):
        for h, sl in enumerate(heads):
            o_ref[0, s * ts:(s + 1) * ts, sl] = acc_ref[s * SB_HEADS + h].astype(BF16)


def _attn_prompt_call(q, kb, vb, *, tq, ts):
    b, n, w = q.shape
    nchain = SB_HEADS * (tq // ts)
    qspec = pl.BlockSpec((1, tq, w), lambda bi, i: (bi, i, 0))
    kvspec = pl.BlockSpec((1, n, w), lambda bi, i: (bi, 0, 0))
    return pl.pallas_call(
        functools.partial(_attn_prompt_kernel, tq=tq, ts=ts),
        grid=(b, n // tq),
        in_specs=[qspec, kvspec, kvspec],
        out_specs=qspec,
        out_shape=jax.ShapeDtypeStruct((b, n, w), BF16),
        scratch_shapes=[pltpu.VMEM((nchain, ts, SB_HEAD_DIM), F32),
                        pltpu.VMEM((nchain, ts, 128), F32),
                        pltpu.VMEM((nchain * ts, 2 * ts), BF16),
                        pltpu.VMEM((nchain, ts, ts), F32)],
        compiler_params=pltpu.CompilerParams(
            dimension_semantics=("arbitrary", "arbitrary"), vmem_limit_bytes=VMEM_LIMIT),
        name="attn_prompt",
    )(q, kb, vb)


def _attn_sample_kernel(q_ref, k_ref, v_ref, ck_hbm, cv_hbm, o_ref, acc_ref, carry_ref, hl_ref, lb_ref,
                        kbuf, vbuf, sem, *, layer, tq, tk, nblk):
    b = pl.program_id(0)

    def copies(jb, slot):
        cols = pl.ds(jb * tk, tk)
        return (pltpu.make_async_copy(ck_hbm.at[layer, b, :, :, cols], kbuf.at[slot], sem.at[0, slot]),
                pltpu.make_async_copy(cv_hbm.at[layer, b, :, :, cols], vbuf.at[slot], sem.at[1, slot]))

    for c in copies(nblk - 1, 0):
        c.start()

    _sb_heads(q_ref, lambda h, sl: (k_ref[0, :, sl], v_ref[0, :, sl]), acc_ref, carry_ref,
              _diag_mask(tq), _suffix_matrix(tq), True)
    m2 = _suffix_matrix(tk)

    def body(st):
        jb, _ = st
        slot = (nblk - 1 - jb) & 1
        for c in copies(jb, slot):
            c.wait()

        @pl.when(jb > 0)
        def _():
            for c in copies(jb - 1, 1 - slot):
                c.start()

        chains = [(lambda sl=sl: q_ref[0, :, sl],
                   lambda h=h: kbuf[slot, h].astype(BF16),
                   lambda h=h: vbuf[slot, h].astype(BF16)) for h, sl in enumerate(_head_slices())]
        _sb_chains_batched(chains, acc_ref, carry_ref, hl_ref, lb_ref, None, m2, False, keys_minor=True)
        return jb - 1, _all_underflowed(carry_ref)

    jb_end, _ = lax.while_loop(lambda st: (st[0] >= 0) & (st[1] == 0), body,
                               (jnp.int32(nblk - 1), jnp.int32(0)))

    @pl.when(jb_end >= 0)
    def _():
        for c in copies(jb_end, (nblk - 1 - jb_end) & 1):
            c.wait()

    _store_heads(o_ref, acc_ref)


def _attn_sample_call(q, kb, vb, cache_kt, cache_vt, layer, *, tk):
    b, n, w = q.shape
    past = cache_kt.shape[4]
    tk = min(tk, past)
    assert past % tk == 0 and past >= tk
    newspec = pl.BlockSpec((1, n, w), lambda bi: (bi, 0, 0))
    anyspec = pl.BlockSpec(memory_space=pl.ANY)
    return pl.pallas_call(
        functools.partial(_attn_sample_kernel, layer=layer, tq=n, tk=tk, nblk=past // tk),
        grid=(b,),
        in_specs=[newspec, newspec, newspec, anyspec, anyspec],
        out_specs=newspec,
        out_shape=jax.ShapeDtypeStruct((b, n, w), BF16),
        scratch_shapes=[pltpu.VMEM((SB_HEADS, n, SB_HEAD_DIM), F32),
                        pltpu.VMEM((SB_HEADS, n, 128), F32),
                        pltpu.VMEM((SB_HEADS * n, 2 * tk), BF16),
                        pltpu.VMEM((SB_HEADS, n, tk), F32),
                        pltpu.VMEM((2, SB_HEADS, SB_HEAD_DIM, tk), F32),
                        pltpu.VMEM((2, SB_HEADS, SB_HEAD_DIM, tk), F32),
                        pltpu.SemaphoreType.DMA((2, 2))],
        compiler_params=pltpu.CompilerParams(
            dimension_semantics=("arbitrary",), vmem_limit_bytes=VMEM_LIMIT),
        name="attn_sample",
    )(q, kb, vb, cache_kt, cache_vt)


def _outproj_kernel(x_ref, po_ref, at_ref, mod_ref, w_out_ref, g_ref, b_ref, x1_ref, mix_ref, *, bt, tn):
    if bt == 1:
        rc = _pick(tn, ROW_CHUNK)
        g1 = mod_ref[0, 2:3, :]
        for r0 in range(0, tn, rc):
            rows = slice(r0, r0 + rc)
            mix = (_bdot(po_ref[0, rows, :], w_out_ref[0:POOL_WIDTH, :])
                   + _bdot(at_ref[0, rows, :], w_out_ref[POOL_WIDTH:, :]))
            y = ALPHA * x_ref[0, rows, :] + (1.0 + g1) * mix
            x1_ref[0, rows, :] = _norm(y) * g_ref[...] + b_ref[...]
        return
    for s in range(bt):
        rows = slice(s * tn, (s + 1) * tn)
        mix_ref[rows, :] = (_bdot(po_ref[s], w_out_ref[0:POOL_WIDTH, :])
                            + _bdot(at_ref[s], w_out_ref[POOL_WIDTH:, :]))
    for s in range(bt):
        rows = slice(s * tn, (s + 1) * tn)
        g1 = mod_ref[s, 2:3, :]
        y = ALPHA * x_ref[s] + (1.0 + g1) * mix_ref[rows, :]
        x1_ref[s] = _norm(y) * g_ref[...] + b_ref[...]


def _outproj_call(x, po, at, mod, w_out, ln_g, ln_b, *, bt, tn):
    b, n, d = x.shape
    tok = lambda width: pl.BlockSpec((bt, tn, width), lambda bi, j: (bi, j, 0))
    const2 = lambda s: pl.BlockSpec(s, lambda bi, j: (0, 0))
    return pl.pallas_call(
        functools.partial(_outproj_kernel, bt=bt, tn=tn),
        grid=(b // bt, n // tn),
        in_specs=[tok(d), tok(POOL_WIDTH), tok(SB_WIDTH),
                  pl.BlockSpec((bt, 6, d), lambda bi, j: (bi, 0, 0)),
                  const2(w_out.shape), const2((1, d)), const2((1, d))],
        out_specs=tok(d),
        out_shape=jax.ShapeDtypeStruct((b, n, d), F32),
        scratch_shapes=[pltpu.VMEM((bt * tn, d), F32)],
        compiler_params=pltpu.CompilerParams(
            dimension_semantics=("arbitrary", "arbitrary"), vmem_limit_bytes=VMEM_LIMIT),
        name="outproj",
    )(x, po, at, mod, w_out, ln_g, ln_b)


def _route(logits):
    col = lax.broadcasted_iota(jnp.int32, logits.shape, 1)
    colf = col.astype(F32)
    is_group = col < N_GROUPS
    gl = jnp.where(is_group, logits, NEG_BIG)
    gmax = jnp.max(gl, axis=1, keepdims=True)
    gsum = jnp.sum(jnp.where(is_group, jnp.exp(gl - gmax), 0.0), axis=1, keepdims=True)
    pg = 1.0 / gsum
    gidx = jnp.min(jnp.where(gl == gmax, colf, 1e9), axis=1, keepdims=True)
    egrp = ((col - N_GROUPS) >> 2).astype(F32)
    in_group = (col >= N_GROUPS) & (col < N_GROUPS + N_EXPERTS) & (egrp == gidx)
    el = jnp.where(in_group, logits, NEG_BIG)
    m1 = jnp.max(el, axis=1, keepdims=True)
    i1 = jnp.min(jnp.where(el == m1, colf, 1e9), axis=1, keepdims=True)
    el2 = jnp.where(colf == i1, NEG_BIG, el)
    m2 = jnp.max(el2, axis=1, keepdims=True)
    i2 = jnp.min(jnp.where(el2 == m2, colf, 1e9), axis=1, keepdims=True)
    r = jnp.exp(m2 - m1)
    w1 = pg / (1.0 + r)
    w2 = pg * r / (1.0 + r)
    return jnp.where(colf == i1, w1, 0.0) + jnp.where(colf == i2, w2, 0.0)


def _moe_kernel(x1_ref, mod_ref, wr_ref, br_ref, wup_ref, wdn_ref, g_ref, b_ref, o_ref,
                hn_ref, gate_ref, acc_ref, *, bt, tn):
    g = pl.program_id(2)

    @pl.when(g == 0)
    def _():
        for s in range(bt):
            sh2 = mod_ref[s, 3:4, :]
            sc2 = mod_ref[s, 4:5, :]
            hn_ref[s * tn:(s + 1) * tn, :] = (_norm(x1_ref[s]) * (1.0 + sc2) + sh2).astype(BF16)
        logits = _bdot(hn_ref[...], wr_ref[...]) + br_ref[...]
        gate_ref[...] = _route(logits)

    hn = hn_ref[...]
    gate = gate_ref[...]
    col = lax.broadcasted_iota(jnp.int32, gate.shape, 1)
    y = None
    for e in range(EXPERTS_PER_GROUP):
        hid = _bdot(hn, wup_ref[e])
        a = hid[:, 0:D_EXPERT]
        u = hid[:, D_EXPERT:]
        lane = N_GROUPS + g * EXPERTS_PER_GROUP + e
        gcol = jnp.sum(jnp.where(col == lane, gate, 0.0), axis=1, keepdims=True)
        act = a * (1.0 / (1.0 + jnp.exp(-a))) * u * gcol
        part = _bdot(act.astype(BF16), wdn_ref[e])
        y = part if y is None else y + part

    @pl.when(g == 0)
    def _():
        acc_ref[...] = y

    @pl.when(g > 0)
    def _():
        acc_ref[...] += y

    @pl.when(g == pl.num_programs(2) - 1)
    def _():
        for s in range(bt):
            g2 = mod_ref[s, 5:6, :]
            y2 = ALPHA * x1_ref[s] + (1.0 + g2) * acc_ref[s * tn:(s + 1) * tn, :]
            o_ref[s] = _norm(y2) * g_ref[...] + b_ref[...]


def _moe_call(x1, mod, w_r, b_r, w_up, w_down, ln_g, ln_b, *, bt, tn):
    b, n, d = x1.shape
    tm = bt * tn
    tok = pl.BlockSpec((bt, tn, d), lambda bi, j, g: (bi, j, 0))
    const2 = lambda s: pl.BlockSpec(s, lambda bi, j, g: (0, 0))
    grp = lambda w: pl.BlockSpec((EXPERTS_PER_GROUP,) + w.shape[1:], lambda bi, j, g: (g, 0, 0))
    return pl.pallas_call(
        functools.partial(_moe_kernel, bt=bt, tn=tn),
        grid=(b // bt, n // tn, N_GROUPS),
        in_specs=[tok,
                  pl.BlockSpec((bt, 6, d), lambda bi, j, g: (bi, 0, 0)),
                  const2(w_r.shape), const2((1, ROUTER_LANES)),
                  grp(w_up), grp(w_down),
                  const2((1, d)), const2((1, d))],
        out_specs=tok,
        out_shape=jax.ShapeDtypeStruct((b, n, d), F32),
        scratch_shapes=[pltpu.VMEM((tm, d), BF16),
                        pltpu.VMEM((tm, ROUTER_LANES), F32),
                        pltpu.VMEM((tm, d), F32)],
        compiler_params=pltpu.CompilerParams(
            dimension_semantics=("arbitrary", "arbitrary", "arbitrary"), vmem_limit_bytes=VMEM_LIMIT),
        name="moe",
    )(x1, mod, w_r, b_r, w_up, w_down, ln_g, ln_b)


def _pick(n, pref):
    t = min(n, pref)
    while n % t:
        t //= 2
    return t


def kernel(x_prompt, x_sample, c_prompt, c_sample, cache_k, cache_v, state_pool, w_mod, b_mod, w_in, w_pool, pool_scale, w_out, ln1_g, ln1_b, ln2_g, ln2_b, w_group, b_group, w_router, b_router, w_up, w_down):
    depth = w_mod.shape[0]
    bp, seq, d = x_prompt.shape
    bs, dseq, _ = x_sample.shape
    past = cache_k.shape[2]

    w_mod_b = w_mod.astype(BF16)
    w_in_b = w_in.astype(BF16)
    w_pool_b = w_pool.astype(BF16)
    w_out_b = w_out.astype(BF16)
    w_up_b = w_up.astype(BF16)
    w_down_b = w_down.astype(BF16)
    pad = jnp.zeros((depth, d, ROUTER_LANES - N_GROUPS - N_EXPERTS), F32)
    w_r = jnp.concatenate([w_group, w_router, pad], axis=-1).astype(BF16)
    b_r = jnp.concatenate([b_group, b_router, pad[:, 0, :]], axis=-1).reshape(depth, 1, ROUTER_LANES)

    cache_kt = jnp.transpose(cache_k, (0, 1, 3, 4, 2))
    cache_vt = jnp.transpose(cache_v, (0, 1, 3, 4, 2))

    mod_all = _mod_call(jnp.concatenate([c_prompt, c_sample], axis=0), w_mod_b, b_mod)
    mod_p = mod_all[:, :bp].reshape(depth, bp, 6, d)
    mod_s = mod_all[:, bp:].reshape(depth, bs, 6, d)

    tm_p = _pick(seq, 1024)
    tq_p = _pick(seq, 256)
    tn_p = _pick(seq, 1024)
    bt_s = _pick(bs, max(1, 1024 // dseq))

    def run_layer(l, x, mod, hist16, pos0, is_prompt):
        q, k, v, kb, vb, po, new_pool = _inproj_call(
            x, mod, hist16, w_in_b[l], w_pool_b[l], pool_scale[l].reshape(1, POOL_WIDTH),
            tm=tm_p if is_prompt else dseq, pos0=pos0)
        if is_prompt:
            at = _attn_prompt_call(q, kb, vb, tq=tq_p, ts=min(tq_p, 128))
            bt, tn = 1, tn_p
        else:
            at = _attn_sample_call(q, kb, vb, cache_kt, cache_vt, l, tk=256)
            bt, tn = bt_s, dseq
        x1 = _outproj_call(x, po, at, mod, w_out_b[l], ln1_g[l].reshape(1, d), ln1_b[l].reshape(1, d),
                           bt=bt, tn=tn)
        x2 = _moe_call(x1, mod, w_r[l], b_r[l], w_up_b[l], w_down_b[l],
                       ln2_g[l].reshape(1, d), ln2_b[l].reshape(1, d), bt=bt, tn=tn)
        b, n = x.shape[0], x.shape[1]
        return (x2, k.reshape(b, n, SB_HEADS, SB_HEAD_DIM), v.reshape(b, n, SB_HEADS, SB_HEAD_DIM),
                new_pool[:, HIST_ROWS - POOL_HIST:])

    y = x_prompt
    kp, vp, pp = [], [], []
    zero_hist = jnp.zeros((bp, HIST_ROWS, POOL_WIDTH), F32)
    for l in range(depth):
        y, k_new, v_new, p_new = run_layer(l, y, mod_p[l], zero_hist, 0, True)
        kp.append(k_new); vp.append(v_new); pp.append(p_new)
    y_prompt = y

    y = x_sample
    kd, vd, pd = [], [], []
    hist_s = jnp.pad(state_pool, ((0, 0), (0, 0), (HIST_ROWS - POOL_HIST, 0), (0, 0)))
    for l in range(depth):
        y, k_new, v_new, p_new = run_layer(l, y, mod_s[l], hist_s[l], past, False)
        kd.append(k_new); vd.append(v_new); pd.append(p_new)
    y_sample = y

    return (y_prompt, y_sample, jnp.stack(kp), jnp.stack(vp), jnp.stack(pp),
            jnp.stack(kd), jnp.stack(vd), jnp.stack(pd))
```

```python
import functools

import jax
import jax.numpy as jnp
from jax import lax
from jax.experimental import pallas as pl
from jax.experimental.pallas import tpu as pltpu

F32 = jnp.float32
BF16 = jnp.bfloat16

D_MODEL = 1024
POOL_WINDOWS = (2, 4, 8, 16)
POOL_WIDTH = 512
POOL_GROUP_DIM = 128
POOL_HIST = 15
HIST_ROWS = 16
SB_WIDTH = 512
SB_HEADS = 8
SB_HEAD_DIM = 64
N_GROUPS = 4
EXPERTS_PER_GROUP = 4
N_EXPERTS = 16
D_EXPERT = 256
ROUTER_LANES = 128
DEPTH = 2
ALPHA = (2 * DEPTH) ** 0.25
LN_EPS = 1e-5
QK_SCALE = 1.0 / 8.0
LOG_UNDERFLOW = -104.0
NEG_BIG = -1e30
VMEM_LIMIT = 56 * 1024 * 1024
ROW_CHUNK = 256


def _norm(x):
    mu = jnp.mean(x, axis=-1, keepdims=True)
    xc = x - mu
    var = jnp.mean(xc * xc, axis=-1, keepdims=True)
    return xc * lax.rsqrt(var + LN_EPS)


def _bdot(a, b):
    return jnp.dot(a, b, preferred_element_type=F32)


def _mod_kernel(c_ref, w_ref, b_ref, o_ref):
    o_ref[0] = _bdot(c_ref[...].astype(BF16), w_ref[0]) + b_ref[0]


def _mod_call(c_all, w_mod, b_mod):
    depth, d, n6 = w_mod.shape
    bc = c_all.shape[0]
    tn = 1536
    return pl.pallas_call(
        _mod_kernel,
        grid=(depth, n6 // tn),
        in_specs=[
            pl.BlockSpec((bc, d), lambda l, j: (0, 0)),
            pl.BlockSpec((1, d, tn), lambda l, j: (l, 0, j)),
            pl.BlockSpec((1, 1, tn), lambda l, j: (l, 0, j)),
        ],
        out_specs=pl.BlockSpec((1, bc, tn), lambda l, j: (l, 0, j)),
        out_shape=jax.ShapeDtypeStruct((depth, bc, n6), F32),
        name="mod",
    )(c_all, w_mod, b_mod.reshape(depth, 1, n6))


def _inproj_kernel(x_ref, mod_ref, hist_ref, w_in_ref, w_pool_ref, ps_ref,
                   q_ref, k_ref, v_ref, kb_ref, vb_ref, po_ref, np_ref, ue_ref, *, tm, rc, pos0):
    j = pl.program_id(1)
    sh1 = mod_ref[0, 0:1, :]
    sc1 = mod_ref[0, 1:2, :]

    @pl.when(j == 0)
    def _():
        ue_ref[0:HIST_ROWS, :] = hist_ref[0]

    for r0 in range(0, tm, rc):
        rows = slice(r0, r0 + rc)
        hn = (_norm(x_ref[0, rows, :]) * (1.0 + sc1) + sh1).astype(BF16)
        proj = _bdot(hn, w_in_ref[...])
        u = proj[:, 0:POOL_WIDTH]
        q = proj[:, POOL_WIDTH:POOL_WIDTH + SB_WIDTH]
        k = proj[:, POOL_WIDTH + SB_WIDTH:POOL_WIDTH + 2 * SB_WIDTH]
        v = proj[:, POOL_WIDTH + 2 * SB_WIDTH:]
        q_ref[0, rows, :] = (q * QK_SCALE).astype(BF16)
        k_ref[0, rows, :] = k
        v_ref[0, rows, :] = v
        kb_ref[0, rows, :] = k.astype(BF16)
        vb_ref[0, rows, :] = v.astype(BF16)
        ue_ref[HIST_ROWS + r0:HIST_ROWS + r0 + rc, :] = u
        pos = pos0 + j * tm + r0 + lax.broadcasted_iota(jnp.int32, (rc, 1), 0)
        outs = []
        for g, w in enumerate(POOL_WINDOWS):
            c0 = g * POOL_GROUP_DIM
            c1 = c0 + POOL_GROUP_DIM
            acc = u[:, c0:c1]
            for dlt in range(1, w):
                acc = acc + ue_ref[HIST_ROWS + r0 - dlt:HIST_ROWS + r0 - dlt + rc, c0:c1]
            cnt = jnp.minimum(w, pos + 1).astype(F32)
            pooled = acc / cnt - u[:, c0:c1]
            mixed = _bdot(pooled.astype(BF16), w_pool_ref[g])
            outs.append(mixed * ps_ref[:, c0:c1])
        po_ref[0, rows, :] = jnp.concatenate(outs, axis=1).astype(BF16)

    tail = ue_ref[tm:tm + HIST_ROWS, :]
    ue_ref[0:HIST_ROWS, :] = tail

    @pl.when(j == pl.num_programs(1) - 1)
    def _():
        np_ref[0] = tail


def _inproj_call(x, mod, hist16, w_in, w_pool, pool_scale, *, tm, pos0):
    b, n, d = x.shape
    nproj = w_in.shape[1]
    tok = lambda width: pl.BlockSpec((1, tm, width), lambda bi, j: (bi, j, 0))
    per_b = lambda rows, width: pl.BlockSpec((1, rows, width), lambda bi, j: (bi, 0, 0))
    const2 = lambda s: pl.BlockSpec(s, lambda bi, j: (0, 0))
    return pl.pallas_call(
        functools.partial(_inproj_kernel, tm=tm, rc=_pick(tm, ROW_CHUNK), pos0=pos0),
        grid=(b, n // tm),
        in_specs=[
            tok(d),
            per_b(6, d),
            per_b(HIST_ROWS, POOL_WIDTH),
            const2((d, nproj)),
            pl.BlockSpec(w_pool.shape, lambda bi, j: (0, 0, 0)),
            const2((1, POOL_WIDTH)),
        ],
        out_specs=[tok(SB_WIDTH), tok(SB_WIDTH), tok(SB_WIDTH), tok(SB_WIDTH), tok(SB_WIDTH),
                   tok(POOL_WIDTH), per_b(HIST_ROWS, POOL_WIDTH)],
        out_shape=[
            jax.ShapeDtypeStruct((b, n, SB_WIDTH), BF16),
            jax.ShapeDtypeStruct((b, n, SB_WIDTH), F32),
            jax.ShapeDtypeStruct((b, n, SB_WIDTH), F32),
            jax.ShapeDtypeStruct((b, n, SB_WIDTH), BF16),
            jax.ShapeDtypeStruct((b, n, SB_WIDTH), BF16),
            jax.ShapeDtypeStruct((b, n, POOL_WIDTH), BF16),
            jax.ShapeDtypeStruct((b, HIST_ROWS, POOL_WIDTH), F32),
        ],
        scratch_shapes=[pltpu.VMEM((tm + HIST_ROWS, POOL_WIDTH), F32)],
        compiler_params=pltpu.CompilerParams(
            dimension_semantics=("arbitrary", "arbitrary"), vmem_limit_bytes=VMEM_LIMIT),
        name="inproj",
    )(x, mod, hist16, w_in, w_pool, pool_scale)


def _suffix_matrix(tk):
    r = lax.broadcasted_iota(jnp.int32, (2 * tk, tk), 0)
    c = lax.broadcasted_iota(jnp.int32, (2 * tk, tk), 1)
    r = jnp.where(r >= tk, r - tk, r)
    return jnp.where(r > c, 1.0, 0.0).astype(BF16)


def _split_hi_lo(x):
    hi = x.astype(BF16)
    return hi, (x - hi.astype(F32)).astype(BF16)


def _sb_chains_batched(chains, acc_ref, carry_ref, hl_ref, lb_ref, mask, m2, first, keys_minor=False):
    tq = acc_ref.shape[1]
    tk = m2.shape[1]
    k_contract = 0 if keys_minor else 1
    tots = []
    for c, (get_q, get_k, _) in enumerate(chains):
        z = lax.dot_general(get_q(), get_k(), (((1,), (k_contract,)), ((), ())),
                            preferred_element_type=F32)
        l1p = jnp.log(1.0 + jnp.exp(-jnp.abs(z)))
        log_beta = jnp.minimum(z, 0.0) - l1p
        log_1m = log_beta - z
        if mask is not None:
            log_1m = jnp.where(mask, log_1m, 0.0)
        hi, lo = _split_hi_lo(log_1m)
        hl_ref[c * tq:(c + 1) * tq, :] = jnp.concatenate([hi, lo], axis=1)
        lb_ref[c] = log_beta
        tots.append(jnp.broadcast_to(jnp.sum(log_1m, axis=1, keepdims=True), (tq, 128)))
    suffix = _bdot(hl_ref[...], m2)
    for c, (_, _, get_v) in enumerate(chains):
        arg = lb_ref[c] + suffix[c * tq:(c + 1) * tq]
        if not first:
            arg = arg + (jnp.concatenate([carry_ref[c]] * (tk // 128), axis=1) if tk > 128 else carry_ref[c])
        a = jnp.exp(arg)
        if mask is not None:
            a = jnp.where(mask, a, 0.0)
        pv = lax.dot_general(a.astype(BF16), get_v(), (((1,), (1 - k_contract,)), ((), ())),
                             preferred_element_type=F32)
        if first:
            acc_ref[c] = pv
            carry_ref[c] = tots[c]
        else:
            acc_ref[c] += pv
            carry_ref[c] += tots[c]


def _head_slices():
    return [slice(h * SB_HEAD_DIM, (h + 1) * SB_HEAD_DIM) for h in range(SB_HEADS)]


def _all_underflowed(carry_ref):
    return (jnp.max(carry_ref[...]) < LOG_UNDERFLOW).astype(jnp.int32)


def _store_heads(o_ref, acc_ref):
    for h in range(SB_HEADS):
        o_ref[0, :, h * SB_HEAD_DIM:(h + 1) * SB_HEAD_DIM] = acc_ref[h].astype(BF16)


def _diag_mask(t):
    row = lax.broadcasted_iota(jnp.int32, (t, t), 0)
    col = lax.broadcasted_iota(jnp.int32, (t, t), 1)
    return col < row


def _attn_prompt_kernel(q_ref, k_ref, v_ref, o_ref, acc_ref, carry_ref, hl_ref, lb_ref, *, tq, ts):
    i = pl.program_id(1)
    nsub = tq // ts
    base = i * nsub
    m2 = _suffix_matrix(ts)
    heads = _head_slices()

    def chains(blocks):
        out = []
        for s in range(nsub):
            k0 = pl.multiple_of(blocks[s] * ts, ts)
            rows = slice(s * ts, (s + 1) * ts)
            for sl in heads:
                out.append((lambda rows=rows, sl=sl: q_ref[0, rows, sl],
                            lambda k0=k0, sl=sl: k_ref[0, pl.ds(k0, ts), sl],
                            lambda k0=k0, sl=sl: v_ref[0, pl.ds(k0, ts), sl]))
        return out

    _sb_chains_batched(chains([base + s for s in range(nsub)]), acc_ref, carry_ref, hl_ref, lb_ref,
                       _diag_mask(ts), m2, True)

    def body(st):
        k, _ = st
        blocks = []
        for s in range(nsub):
            blk = base + s - k
            if s < nsub - 1:
                @pl.when(blk < 0)
                def _(s=s):
                    carry_ref[s * SB_HEADS:(s + 1) * SB_HEADS] = jnp.full((SB_HEADS, ts, 128), NEG_BIG, F32)
            blocks.append(jnp.maximum(blk, 0))
        _sb_chains_batched(chains(blocks), acc_ref, carry_ref, hl_ref, lb_ref, None, m2, False)
        return k + 1, _all_underflowed(carry_ref)

    lax.while_loop(lambda st: (base + nsub - 1 - st[0] >= 0) & (st[1] == 0), body,
                   (jnp.int32(1), jnp.int32(0)))
    for s in range(nsub):
        for h, sl in enumerate(heads):
            o_ref[0, s * ts:(s + 1) * ts, sl] = acc_ref[s * SB_HEADS + h].astype(BF16)


def _attn_prompt_call(q, kb, vb, *, tq, ts):
    b, n, w = q.shape
    nchain = SB_HEADS * (tq // ts)
    qspec = pl.BlockSpec((1, tq, w), lambda bi, i: (bi, i, 0))
    kvspec = pl.BlockSpec((1, n, w), lambda bi, i: (bi, 0, 0))
    return pl.pallas_call(
        functools.partial(_attn_prompt_kernel, tq=tq, ts=ts),
        grid=(b, n // tq),
        in_specs=[qspec, kvspec, kvspec],
        out_specs=qspec,
        out_shape=jax.ShapeDtypeStruct((b, n, w), BF16),
        scratch_shapes=[pltpu.VMEM((nchain, ts, SB_HEAD_DIM), F32),
                        pltpu.VMEM((nchain, ts, 128), F32),
                        pltpu.VMEM((nchain * ts, 2 * ts), BF16),
                        pltpu.VMEM((nchain, ts, ts), F32)],
        compiler_params=pltpu.CompilerParams(
            dimension_semantics=("arbitrary", "arbitrary"), vmem_limit_bytes=VMEM_LIMIT),
        name="attn_prompt",
    )(q, kb, vb)


def _attn_sample_kernel(q_ref, k_ref, v_ref, ck_hbm, cv_hbm, o_ref, acc_ref, carry_ref, hl_ref, lb_ref,
                        kbuf, vbuf, sem, *, layer, tq, tk, nblk):
    b = pl.program_id(0)
    first_slot = b & 1

    def copies(stream, jb, slot):
        cols = pl.ds(jb * tk, tk)
        return (pltpu.make_async_copy(ck_hbm.at[layer, stream, :, :, cols], kbuf.at[slot], sem.at[0, slot]),
                pltpu.make_async_copy(cv_hbm.at[layer, stream, :, :, cols], vbuf.at[slot], sem.at[1, slot]))

    def later_slot(jb):
        return 2 + ((nblk - 1 - jb) & 1)

    @pl.when(b == 0)
    def _():
        for c in copies(b, nblk - 1, first_slot):
            c.start()

    @pl.when(b + 1 < pl.num_programs(0))
    def _():
        for c in copies(b + 1, nblk - 1, 1 - first_slot):
            c.start()

    tn = 128 * pl.cdiv(tq, 128)
    row = lax.broadcasted_iota(jnp.int32, (tq, tn), 0)
    col = lax.broadcasted_iota(jnp.int32, (tq, tn), 1)
    pad = jnp.zeros((tn - tq, SB_HEAD_DIM), BF16)
    padded = (lambda x: jnp.concatenate([x, pad], axis=0)) if tn > tq else (lambda x: x)
    new_chains = [(lambda sl=sl: q_ref[0, :, sl],
                   lambda sl=sl: padded(k_ref[0, :, sl]),
                   lambda sl=sl: padded(v_ref[0, :, sl])) for sl in _head_slices()]
    _sb_chains_batched(new_chains, acc_ref, carry_ref, hl_ref.at[:, 0:2 * tn], lb_ref.at[:, :, 0:tn],
                       col < row, _suffix_matrix(tn), True)
    m2 = _suffix_matrix(tk)

    def body(st):
        jb, _ = st
        slot = jnp.where(jb == nblk - 1, first_slot, later_slot(jb))
        for c in copies(b, jb, slot):
            c.wait()

        @pl.when(jb > 0)
        def _():
            for c in copies(b, jb - 1, later_slot(jb - 1)):
                c.start()

        chains = [(lambda sl=sl: q_ref[0, :, sl],
                   lambda h=h: kbuf[slot, h].astype(BF16),
                   lambda h=h: vbuf[slot, h].astype(BF16)) for h, sl in enumerate(_head_slices())]
        _sb_chains_batched(chains, acc_ref, carry_ref, hl_ref, lb_ref, None, m2, False, keys_minor=True)
        return jb - 1, _all_underflowed(carry_ref)

    jb_end, _ = lax.while_loop(lambda st: (st[0] >= 0) & (st[1] == 0), body,
                               (jnp.int32(nblk - 1), jnp.int32(0)))

    @pl.when(jb_end >= 0)
    def _():
        for c in copies(b, jb_end, later_slot(jb_end)):
            c.wait()

    _store_heads(o_ref, acc_ref)


def _attn_sample_call(q, kb, vb, cache_kt, cache_vt, layer, *, tk):
    b, n, w = q.shape
    past = cache_kt.shape[4]
    tk = min(tk, past)
    assert past % tk == 0 and past >= tk
    newspec = pl.BlockSpec((1, n, w), lambda bi: (bi, 0, 0))
    anyspec = pl.BlockSpec(memory_space=pl.ANY)
    return pl.pallas_call(
        functools.partial(_attn_sample_kernel, layer=layer, tq=n, tk=tk, nblk=past // tk),
        grid=(b,),
        in_specs=[newspec, newspec, newspec, anyspec, anyspec],
        out_specs=newspec,
        out_shape=jax.ShapeDtypeStruct((b, n, w), BF16),
        scratch_shapes=[pltpu.VMEM((SB_HEADS, n, SB_HEAD_DIM), F32),
                        pltpu.VMEM((SB_HEADS, n, 128), F32),
                        pltpu.VMEM((SB_HEADS * n, 2 * tk), BF16),
                        pltpu.VMEM((SB_HEADS, n, tk), F32),
                        pltpu.VMEM((4, SB_HEADS, SB_HEAD_DIM, tk), F32),
                        pltpu.VMEM((4, SB_HEADS, SB_HEAD_DIM, tk), F32),
                        pltpu.SemaphoreType.DMA((2, 4))],
        compiler_params=pltpu.CompilerParams(
            dimension_semantics=("arbitrary",), vmem_limit_bytes=VMEM_LIMIT),
        name="attn_sample",
    )(q, kb, vb, cache_kt, cache_vt)


def _outproj_kernel(x_ref, po_ref, at_ref, mod_ref, w_out_ref, g_ref, b_ref, x1_ref, mix_ref, *, bt, tn):
    if bt == 1:
        rc = _pick(tn, ROW_CHUNK)
        g1 = mod_ref[0, 2:3, :]
        for r0 in range(0, tn, rc):
            rows = slice(r0, r0 + rc)
            mix = (_bdot(po_ref[0, rows, :], w_out_ref[0:POOL_WIDTH, :])
                   + _bdot(at_ref[0, rows, :], w_out_ref[POOL_WIDTH:, :]))
            y = ALPHA * x_ref[0, rows, :] + (1.0 + g1) * mix
            x1_ref[0, rows, :] = _norm(y) * g_ref[...] + b_ref[...]
        return
    for s in range(bt):
        rows = slice(s * tn, (s + 1) * tn)
        mix_ref[rows, :] = (_bdot(po_ref[s], w_out_ref[0:POOL_WIDTH, :])
                            + _bdot(at_ref[s], w_out_ref[POOL_WIDTH:, :]))
    for s in range(bt):
        rows = slice(s * tn, (s + 1) * tn)
        g1 = mod_ref[s, 2:3, :]
        y = ALPHA * x_ref[s] + (1.0 + g1) * mix_ref[rows, :]
        x1_ref[s] = _norm(y) * g_ref[...] + b_ref[...]


def _outproj_call(x, po, at, mod, w_out, ln_g, ln_b, *, bt, tn):
    b, n, d = x.shape
    tok = lambda width: pl.BlockSpec((bt, tn, width), lambda bi, j: (bi, j, 0))
    const2 = lambda s: pl.BlockSpec(s, lambda bi, j: (0, 0))
    return pl.pallas_call(
        functools.partial(_outproj_kernel, bt=bt, tn=tn),
        grid=(b // bt, n // tn),
        in_specs=[tok(d), tok(POOL_WIDTH), tok(SB_WIDTH),
                  pl.BlockSpec((bt, 6, d), lambda bi, j: (bi, 0, 0)),
                  const2(w_out.shape), const2((1, d)), const2((1, d))],
        out_specs=tok(d),
        out_shape=jax.ShapeDtypeStruct((b, n, d), F32),
        scratch_shapes=[pltpu.VMEM((bt * tn, d), F32)],
        compiler_params=pltpu.CompilerParams(
            dimension_semantics=("arbitrary", "arbitrary"), vmem_limit_bytes=VMEM_LIMIT),
        name="outproj",
    )(x, po, at, mod, w_out, ln_g, ln_b)


def _route(logits):
    col = lax.broadcasted_iota(jnp.int32, logits.shape, 1)
    colf = col.astype(F32)
    is_group = col < N_GROUPS
    gl = jnp.where(is_group, logits, NEG_BIG)
    gmax = jnp.max(gl, axis=1, keepdims=True)
    gsum = jnp.sum(jnp.where(is_group, jnp.exp(gl - gmax), 0.0), axis=1, keepdims=True)
    pg = 1.0 / gsum
    gidx = jnp.min(jnp.where(gl == gmax, colf, 1e9), axis=1, keepdims=True)
    egrp = ((col - N_GROUPS) >> 2).astype(F32)
    in_group = (col >= N_GROUPS) & (col < N_GROUPS + N_EXPERTS) & (egrp == gidx)
    el = jnp.where(in_group, logits, NEG_BIG)
    m1 = jnp.max(el, axis=1, keepdims=True)
    i1 = jnp.min(jnp.where(el == m1, colf, 1e9), axis=1, keepdims=True)
    el2 = jnp.where(colf == i1, NEG_BIG, el)
    m2 = jnp.max(el2, axis=1, keepdims=True)
    i2 = jnp.min(jnp.where(el2 == m2, colf, 1e9), axis=1, keepdims=True)
    r = jnp.exp(m2 - m1)
    w1 = pg / (1.0 + r)
    w2 = pg * r / (1.0 + r)
    return jnp.where(colf == i1, w1, 0.0) + jnp.where(colf == i2, w2, 0.0)


def _moe_prologue(x, sh2, sc2, wr_ref, br_ref):
    hn = (_norm(x) * (1.0 + sc2) + sh2).astype(BF16)
    return hn, _route(_bdot(hn, wr_ref[0]) + br_ref[0])


def _moe_kernel(x1_ref, mod_ref, wr_ref, br_ref, wup_ref, wdn_ref, g_ref, b_ref, o_ref,
                hn_ref, gate_ref, acc_ref, *, bt, tn):
    g = pl.program_id(2)

    @pl.when(g == 0)
    def _():
        for s in range(bt):
            hn, gate = _moe_prologue(x1_ref[s], mod_ref[s, 3:4, :], mod_ref[s, 4:5, :], wr_ref, br_ref)
            hn_ref[s * tn:(s + 1) * tn, :] = hn
            gate_ref[s * tn:(s + 1) * tn, :] = gate

    hn = hn_ref[...]
    gate = gate_ref[...]
    col = lax.broadcasted_iota(jnp.int32, gate.shape, 1)
    y = None
    for e in range(EXPERTS_PER_GROUP):
        hid = _bdot(hn, wup_ref[0, e])
        a = hid[:, 0:D_EXPERT]
        u = hid[:, D_EXPERT:]
        lane = N_GROUPS + g * EXPERTS_PER_GROUP + e
        gcol = jnp.sum(jnp.where(col == lane, gate, 0.0), axis=1, keepdims=True)
        act = a * (1.0 / (1.0 + jnp.exp(-a))) * u * gcol
        part = _bdot(act.astype(BF16), wdn_ref[0, e])
        y = part if y is None else y + part

    @pl.when(g == 0)
    def _():
        acc_ref[...] = y

    @pl.when(g > 0)
    def _():
        acc_ref[...] += y

    @pl.when(g == pl.num_programs(2) - 1)
    def _():
        for s in range(bt):
            g2 = mod_ref[s, 5:6, :]
            y2 = ALPHA * x1_ref[s] + (1.0 + g2) * acc_ref[s * tn:(s + 1) * tn, :]
            o_ref[s] = _norm(y2) * g_ref[0] + b_ref[0]


def _moe_call(x1, mod, w_r, b_r, w_up, w_down, ln_g, ln_b, layer, *, bt, tn):
    b, n, d = x1.shape
    tm = bt * tn
    tok = pl.BlockSpec((bt, tn, d), lambda bi, j, g: (bi, j, 0))
    per_layer = lambda a: pl.BlockSpec((1,) + a.shape[1:], lambda bi, j, g: (layer,) + (0,) * (a.ndim - 1))
    grp = lambda w: pl.BlockSpec((1, EXPERTS_PER_GROUP) + w.shape[2:], lambda bi, j, g: (layer, g, 0, 0))
    return pl.pallas_call(
        functools.partial(_moe_kernel, bt=bt, tn=tn),
        grid=(b // bt, n // tn, N_GROUPS),
        in_specs=[tok,
                  pl.BlockSpec((bt, 6, d), lambda bi, j, g: (bi, 0, 0)),
                  per_layer(w_r), per_layer(b_r),
                  grp(w_up), grp(w_down),
                  per_layer(ln_g), per_layer(ln_b)],
        out_specs=tok,
        out_shape=jax.ShapeDtypeStruct((b, n, d), F32),
        scratch_shapes=[pltpu.VMEM((tm, d), BF16),
                        pltpu.VMEM((tm, ROUTER_LANES), F32),
                        pltpu.VMEM((tm, d), F32)],
        compiler_params=pltpu.CompilerParams(
            dimension_semantics=("arbitrary", "arbitrary", "arbitrary"), vmem_limit_bytes=VMEM_LIMIT),
        name="moe",
    )(x1, mod, w_r, b_r, w_up, w_down, ln_g, ln_b)


def _pick(n, pref):
    t = min(n, pref)
    while n % t:
        t //= 2
    return t


def kernel(x_prompt, x_sample, c_prompt, c_sample, cache_k, cache_v, state_pool, w_mod, b_mod, w_in, w_pool, pool_scale, w_out, ln1_g, ln1_b, ln2_g, ln2_b, w_group, b_group, w_router, b_router, w_up, w_down):
    depth = w_mod.shape[0]
    bp, seq, d = x_prompt.shape
    bs, dseq, _ = x_sample.shape
    past = cache_k.shape[2]

    w_mod_b = w_mod.astype(BF16)
    w_in_b = w_in.astype(BF16)
    w_pool_b = w_pool.astype(BF16)
    w_out_b = w_out.astype(BF16)
    w_up_b = w_up.astype(BF16)
    w_down_b = w_down.astype(BF16)
    pad = jnp.zeros((depth, d, ROUTER_LANES - N_GROUPS - N_EXPERTS), F32)
    w_r = jnp.concatenate([w_group, w_router, pad], axis=-1).astype(BF16)
    b_r = jnp.concatenate([b_group, b_router, pad[:, 0, :]], axis=-1).reshape(depth, 1, ROUTER_LANES)

    cache_kt = jnp.transpose(cache_k, (0, 1, 3, 4, 2))
    cache_vt = jnp.transpose(cache_v, (0, 1, 3, 4, 2))

    ln2_g3 = ln2_g.reshape(depth, 1, d)
    ln2_b3 = ln2_b.reshape(depth, 1, d)

    mod_all = _mod_call(jnp.concatenate([c_prompt, c_sample], axis=0), w_mod_b, b_mod)
    mod_p = mod_all[:, :bp].reshape(depth, bp, 6, d)
    mod_s = mod_all[:, bp:].reshape(depth, bs, 6, d)

    tm_p = _pick(seq, 1024)
    tq_p = _pick(seq, 512)
    tn_p = _pick(seq, 1024)
    bt_s = _pick(bs, max(1, 1024 // dseq))

    def run_layer(l, x, mod, hist16, pos0, is_prompt):
        q, k, v, kb, vb, po, new_pool = _inproj_call(
            x, mod, hist16, w_in_b[l], w_pool_b[l], pool_scale[l].reshape(1, POOL_WIDTH),
            tm=tm_p if is_prompt else dseq, pos0=pos0)
        if is_prompt:
            at = _attn_prompt_call(q, kb, vb, tq=tq_p, ts=min(tq_p, 128))
            bt, tn = 1, tn_p
        else:
            at = _attn_sample_call(q, kb, vb, cache_kt, cache_vt, l, tk=256)
            bt, tn = bt_s, dseq
        x1 = _outproj_call(x, po, at, mod, w_out_b[l], ln1_g[l].reshape(1, d), ln1_b[l].reshape(1, d),
                           bt=bt, tn=tn)
        x2 = _moe_call(x1, mod, w_r, b_r, w_up_b, w_down_b, ln2_g3, ln2_b3, l, bt=bt, tn=tn)
        b, n = x.shape[0], x.shape[1]
        return (x2, k.reshape(b, n, SB_HEADS, SB_HEAD_DIM), v.reshape(b, n, SB_HEADS, SB_HEAD_DIM),
                new_pool[:, HIST_ROWS - POOL_HIST:])

    y = x_prompt
    kp, vp, pp = [], [], []
    zero_hist = jnp.zeros((bp, HIST_ROWS, POOL_WIDTH), F32)
    for l in range(depth):
        y, k_new, v_new, p_new = run_layer(l, y, mod_p[l], zero_hist, 0, True)
        kp.append(k_new); vp.append(v_new); pp.append(p_new)
    y_prompt = y

    y = x_sample
    kd, vd, pd = [], [], []
    hist_s = jnp.pad(state_pool, ((0, 0), (0, 0), (HIST_ROWS - POOL_HIST, 0), (0, 0)))
    for l in range(depth):
        y, k_new, v_new, p_new = run_layer(l, y, mod_s[l], hist_s[l], past, False)
        kd.append(k_new); vd.append(v_new); pd.append(p_new)
    y_sample = y

    return (y_prompt, y_sample, jnp.stack(kp), jnp.stack(vp), jnp.stack(pp),
            jnp.stack(kd), jnp.stack(vd), jnp.stack(pd))
```

```python
import functools

import jax
import jax.numpy as jnp
from jax import lax
from jax.experimental import pallas as pl
from jax.experimental.pallas import tpu as pltpu

F32 = jnp.float32
BF16 = jnp.bfloat16

D_MODEL = 1024
POOL_WINDOWS = (2, 4, 8, 16)
POOL_WIDTH = 512
POOL_GROUP_DIM = 128
POOL_HIST = 15
HIST_ROWS = 16
SB_WIDTH = 512
SB_HEADS = 8
SB_HEAD_DIM = 64
N_GROUPS = 4
EXPERTS_PER_GROUP = 4
N_EXPERTS = 16
D_EXPERT = 256
ROUTER_LANES = 128
DEPTH = 2
ALPHA = (2 * DEPTH) ** 0.25
LN_EPS = 1e-5
QK_SCALE = 1.0 / 8.0
LOG_UNDERFLOW = -104.0
NEG_BIG = -1e30
VMEM_LIMIT = 56 * 1024 * 1024
ROW_CHUNK = 256
TOP_ROWS = 64


def _norm(x):
    mu = jnp.mean(x, axis=-1, keepdims=True)
    xc = x - mu
    var = jnp.mean(xc * xc, axis=-1, keepdims=True)
    return xc * lax.rsqrt(var + LN_EPS)


def _bdot(a, b):
    return jnp.dot(a, b, preferred_element_type=F32)


def _mod_kernel(c_ref, w_ref, b_ref, o_ref):
    o_ref[0] = _bdot(c_ref[...].astype(BF16), w_ref[0]) + b_ref[0]


def _mod_call(c_all, w_mod, b_mod):
    depth, d, n6 = w_mod.shape
    bc = c_all.shape[0]
    tn = 1536
    return pl.pallas_call(
        _mod_kernel,
        grid=(depth, n6 // tn),
        in_specs=[
            pl.BlockSpec((bc, d), lambda l, j: (0, 0)),
            pl.BlockSpec((1, d, tn), lambda l, j: (l, 0, j)),
            pl.BlockSpec((1, 1, tn), lambda l, j: (l, 0, j)),
        ],
        out_specs=pl.BlockSpec((1, bc, tn), lambda l, j: (l, 0, j)),
        out_shape=jax.ShapeDtypeStruct((depth, bc, n6), F32),
        name="mod",
    )(c_all, w_mod, b_mod.reshape(depth, 1, n6))


def _inproj_kernel(x_ref, mod_ref, hist_ref, w_in_ref, w_pool_ref, ps_ref,
                   q_ref, k_ref, v_ref, kb_ref, vb_ref, po_ref, np_ref, ue_ref, *, tm, rc, pos0):
    j = pl.program_id(1)
    sh1 = mod_ref[0, 0:1, :]
    sc1 = mod_ref[0, 1:2, :]

    @pl.when(j == 0)
    def _():
        ue_ref[0:HIST_ROWS, :] = hist_ref[0]

    for r0 in range(0, tm, rc):
        rows = slice(r0, r0 + rc)
        hn = (_norm(x_ref[0, rows, :]) * (1.0 + sc1) + sh1).astype(BF16)
        proj = _bdot(hn, w_in_ref[...])
        u = proj[:, 0:POOL_WIDTH]
        q = proj[:, POOL_WIDTH:POOL_WIDTH + SB_WIDTH]
        k = proj[:, POOL_WIDTH + SB_WIDTH:POOL_WIDTH + 2 * SB_WIDTH]
        v = proj[:, POOL_WIDTH + 2 * SB_WIDTH:]
        q_ref[0, rows, :] = (q * QK_SCALE).astype(BF16)
        k_ref[0, rows, :] = k
        v_ref[0, rows, :] = v
        kb_ref[0, rows, :] = k.astype(BF16)
        vb_ref[0, rows, :] = v.astype(BF16)
        ue_ref[HIST_ROWS + r0:HIST_ROWS + r0 + rc, :] = u
        pos = pos0 + j * tm + r0 + lax.broadcasted_iota(jnp.int32, (rc, 1), 0)
        outs = []
        for g, w in enumerate(POOL_WINDOWS):
            c0 = g * POOL_GROUP_DIM
            c1 = c0 + POOL_GROUP_DIM
            acc = u[:, c0:c1]
            for dlt in range(1, w):
                acc = acc + ue_ref[HIST_ROWS + r0 - dlt:HIST_ROWS + r0 - dlt + rc, c0:c1]
            cnt = jnp.minimum(w, pos + 1).astype(F32)
            pooled = acc / cnt - u[:, c0:c1]
            mixed = _bdot(pooled.astype(BF16), w_pool_ref[g])
            outs.append(mixed * ps_ref[:, c0:c1])
        po_ref[0, rows, :] = jnp.concatenate(outs, axis=1).astype(BF16)

    tail = ue_ref[tm:tm + HIST_ROWS, :]
    ue_ref[0:HIST_ROWS, :] = tail

    @pl.when(j == pl.num_programs(1) - 1)
    def _():
        np_ref[0] = tail


def _inproj_call(x, mod, hist16, w_in, w_pool, pool_scale, *, tm, pos0):
    b, n, d = x.shape
    nproj = w_in.shape[1]
    tok = lambda width: pl.BlockSpec((1, tm, width), lambda bi, j: (bi, j, 0))
    per_b = lambda rows, width: pl.BlockSpec((1, rows, width), lambda bi, j: (bi, 0, 0))
    const2 = lambda s: pl.BlockSpec(s, lambda bi, j: (0, 0))
    return pl.pallas_call(
        functools.partial(_inproj_kernel, tm=tm, rc=_pick(tm, ROW_CHUNK), pos0=pos0),
        grid=(b, n // tm),
        in_specs=[
            tok(d),
            per_b(6, d),
            per_b(HIST_ROWS, POOL_WIDTH),
            const2((d, nproj)),
            pl.BlockSpec(w_pool.shape, lambda bi, j: (0, 0, 0)),
            const2((1, POOL_WIDTH)),
        ],
        out_specs=[tok(SB_WIDTH), tok(SB_WIDTH), tok(SB_WIDTH), tok(SB_WIDTH), tok(SB_WIDTH),
                   tok(POOL_WIDTH), per_b(HIST_ROWS, POOL_WIDTH)],
        out_shape=[
            jax.ShapeDtypeStruct((b, n, SB_WIDTH), BF16),
            jax.ShapeDtypeStruct((b, n, SB_WIDTH), F32),
            jax.ShapeDtypeStruct((b, n, SB_WIDTH), F32),
            jax.ShapeDtypeStruct((b, n, SB_WIDTH), BF16),
            jax.ShapeDtypeStruct((b, n, SB_WIDTH), BF16),
            jax.ShapeDtypeStruct((b, n, POOL_WIDTH), BF16),
            jax.ShapeDtypeStruct((b, HIST_ROWS, POOL_WIDTH), F32),
        ],
        scratch_shapes=[pltpu.VMEM((tm + HIST_ROWS, POOL_WIDTH), F32)],
        compiler_params=pltpu.CompilerParams(
            dimension_semantics=("arbitrary", "arbitrary"), vmem_limit_bytes=VMEM_LIMIT),
        name="inproj",
    )(x, mod, hist16, w_in, w_pool, pool_scale)


def _suffix_matrix(tk):
    r = lax.broadcasted_iota(jnp.int32, (2 * tk, tk), 0)
    c = lax.broadcasted_iota(jnp.int32, (2 * tk, tk), 1)
    r = jnp.where(r >= tk, r - tk, r)
    return jnp.where(r > c, 1.0, 0.0).astype(BF16)


def _split_hi_lo(x):
    hi = x.astype(BF16)
    return hi, (x - hi.astype(F32)).astype(BF16)


def _sb_chains_batched(chains, acc_ref, carry_ref, hl_ref, lb_ref, mask, m2, first, keys_minor=False):
    tq = acc_ref.shape[1]
    tk = m2.shape[1]
    k_contract = 0 if keys_minor else 1
    tots = []
    for c, (get_q, get_k, _) in enumerate(chains):
        z = lax.dot_general(get_q(), get_k(), (((1,), (k_contract,)), ((), ())),
                            preferred_element_type=F32)
        l1p = jnp.log(1.0 + jnp.exp(-jnp.abs(z)))
        log_beta = jnp.minimum(z, 0.0) - l1p
        log_1m = log_beta - z
        if mask is not None:
            log_1m = jnp.where(mask, log_1m, 0.0)
        hi, lo = _split_hi_lo(log_1m)
        hl_ref[c * tq:(c + 1) * tq, :] = jnp.concatenate([hi, lo], axis=1)
        lb_ref[c] = log_beta
        tots.append(jnp.broadcast_to(jnp.sum(log_1m, axis=1, keepdims=True), (tq, 128)))
    suffix = _bdot(hl_ref[...], m2)
    for c, (_, _, get_v) in enumerate(chains):
        arg = lb_ref[c] + suffix[c * tq:(c + 1) * tq]
        if not first:
            arg = arg + (jnp.concatenate([carry_ref[c]] * (tk // 128), axis=1) if tk > 128 else carry_ref[c])
        a = jnp.exp(arg)
        if mask is not None:
            a = jnp.where(mask, a, 0.0)
        pv = lax.dot_general(a.astype(BF16), get_v(), (((1,), (1 - k_contract,)), ((), ())),
                             preferred_element_type=F32)
        if first:
            acc_ref[c] = pv
            carry_ref[c] = tots[c]
        else:
            acc_ref[c] += pv
            carry_ref[c] += tots[c]


def _head_slices():
    return [slice(h * SB_HEAD_DIM, (h + 1) * SB_HEAD_DIM) for h in range(SB_HEADS)]


def _all_underflowed(carry_ref):
    return (jnp.max(carry_ref[...]) < LOG_UNDERFLOW).astype(jnp.int32)


def _store_heads(o_ref, acc_ref):
    for h in range(SB_HEADS):
        o_ref[0, :, h * SB_HEAD_DIM:(h + 1) * SB_HEAD_DIM] = acc_ref[h].astype(BF16)


def _diag_mask(t):
    row = lax.broadcasted_iota(jnp.int32, (t, t), 0)
    col = lax.broadcasted_iota(jnp.int32, (t, t), 1)
    return col < row


def _attn_prompt_kernel(q_ref, k_ref, v_ref, o_ref, acc_ref, carry_ref, hl_ref, lb_ref, *, tq, ts, top):
    i = pl.program_id(1)
    nsub = tq // ts
    base = i * nsub
    m2 = _suffix_matrix(ts)
    heads = _head_slices()

    def chains(blocks, nrows):
        out = []
        for s in range(nsub):
            k0 = pl.multiple_of(blocks[s] * ts, ts)
            rows = slice(s * ts, s * ts + nrows)
            for sl in heads:
                out.append((lambda rows=rows, sl=sl: q_ref[0, rows, sl],
                            lambda k0=k0, sl=sl: k_ref[0, pl.ds(k0, ts), sl],
                            lambda k0=k0, sl=sl: v_ref[0, pl.ds(k0, ts), sl]))
        return out

    _sb_chains_batched(chains([base + s for s in range(nsub)], ts), acc_ref, carry_ref, hl_ref, lb_ref,
                       _diag_mask(ts), m2, True)

    def walk(k, nrows):
        nchain = nsub * SB_HEADS
        blocks = []
        for s in range(nsub):
            blk = base + s - k
            if s < nsub - 1:
                @pl.when(blk < 0)
                def _(s=s):
                    carry_ref[s * SB_HEADS:(s + 1) * SB_HEADS] = jnp.full((SB_HEADS, ts, 128), NEG_BIG, F32)
            blocks.append(jnp.maximum(blk, 0))
        _sb_chains_batched(chains(blocks, nrows), acc_ref.at[:, 0:nrows, :], carry_ref.at[:, 0:nrows, :],
                           hl_ref.at[0:nchain * nrows, :], lb_ref.at[:, 0:nrows, :], None, m2, False)

    def underflowed(lo, hi):
        return (jnp.max(carry_ref[:, lo:hi, :]) < LOG_UNDERFLOW).astype(jnp.int32)

    def more_keys(k):
        return base + nsub - 1 - k >= 0

    def full_body(st):
        k, _, _ = st
        walk(k, ts)
        return k + 1, underflowed(top, ts), underflowed(0, top)

    k, _, top_done = lax.while_loop(lambda st: more_keys(st[0]) & (st[1] == 0), full_body,
                                    (jnp.int32(1), jnp.int32(0), jnp.int32(0)))

    def top_body(st):
        k, _ = st
        walk(k, top)
        return k + 1, underflowed(0, top)

    lax.while_loop(lambda st: more_keys(st[0]) & (st[1] == 0), top_body, (k, top_done))
    for s in range(nsub):
        for h, sl in enumerate(heads):
            o_ref[0, s * ts:(s + 1) * ts, sl] = acc_ref[s * SB_HEADS + h].astype(BF16)


def _attn_prompt_call(q, kb, vb, *, tq, ts):
    b, n, w = q.shape
    nchain = SB_HEADS * (tq // ts)
    qspec = pl.BlockSpec((1, tq, w), lambda bi, i: (bi, i, 0))
    kvspec = pl.BlockSpec((1, n, w), lambda bi, i: (bi, 0, 0))
    return pl.pallas_call(
        functools.partial(_attn_prompt_kernel, tq=tq, ts=ts, top=TOP_ROWS),
        grid=(b, n // tq),
        in_specs=[qspec, kvspec, kvspec],
        out_specs=qspec,
        out_shape=jax.ShapeDtypeStruct((b, n, w), BF16),
        scratch_shapes=[pltpu.VMEM((nchain, ts, SB_HEAD_DIM), F32),
                        pltpu.VMEM((nchain, ts, 128), F32),
                        pltpu.VMEM((nchain * ts, 2 * ts), BF16),
                        pltpu.VMEM((nchain, ts, ts), F32)],
        compiler_params=pltpu.CompilerParams(
            dimension_semantics=("arbitrary", "arbitrary"), vmem_limit_bytes=VMEM_LIMIT),
        name="attn_prompt",
    )(q, kb, vb)


def _attn_sample_kernel(q_ref, k_ref, v_ref, ck_hbm, cv_hbm, o_ref, acc_ref, carry_ref, hl_ref, lb_ref,
                        kbuf, vbuf, sem, *, layer, tq, tk, nblk):
    b = pl.program_id(0)
    first_slot = b & 1

    def copies(stream, jb, slot):
        cols = pl.ds(jb * tk, tk)
        return (pltpu.make_async_copy(ck_hbm.at[layer, stream, :, :, cols], kbuf.at[slot], sem.at[0, slot]),
                pltpu.make_async_copy(cv_hbm.at[layer, stream, :, :, cols], vbuf.at[slot], sem.at[1, slot]))

    def later_slot(jb):
        return 2 + ((nblk - 1 - jb) & 1)

    @pl.when(b == 0)
    def _():
        for c in copies(b, nblk - 1, first_slot):
            c.start()

    @pl.when(b + 1 < pl.num_programs(0))
    def _():
        for c in copies(b + 1, nblk - 1, 1 - first_slot):
            c.start()

    tn = 128 * pl.cdiv(tq, 128)
    row = lax.broadcasted_iota(jnp.int32, (tq, tn), 0)
    col = lax.broadcasted_iota(jnp.int32, (tq, tn), 1)
    pad = jnp.zeros((tn - tq, SB_HEAD_DIM), BF16)
    padded = (lambda x: jnp.concatenate([x, pad], axis=0)) if tn > tq else (lambda x: x)
    new_chains = [(lambda sl=sl: q_ref[0, :, sl],
                   lambda sl=sl: padded(k_ref[0, :, sl]),
                   lambda sl=sl: padded(v_ref[0, :, sl])) for sl in _head_slices()]
    _sb_chains_batched(new_chains, acc_ref, carry_ref, hl_ref.at[:, 0:2 * tn], lb_ref.at[:, :, 0:tn],
                       col < row, _suffix_matrix(tn), True)
    m2 = _suffix_matrix(tk)

    def body(st):
        jb, _ = st
        slot = jnp.where(jb == nblk - 1, first_slot, later_slot(jb))
        for c in copies(b, jb, slot):
            c.wait()

        @pl.when(jb > 0)
        def _():
            for c in copies(b, jb - 1, later_slot(jb - 1)):
                c.start()

        chains = [(lambda sl=sl: q_ref[0, :, sl],
                   lambda h=h: kbuf[slot, h].astype(BF16),
                   lambda h=h: vbuf[slot, h].astype(BF16)) for h, sl in enumerate(_head_slices())]
        _sb_chains_batched(chains, acc_ref, carry_ref, hl_ref, lb_ref, None, m2, False, keys_minor=True)
        return jb - 1, _all_underflowed(carry_ref)

    jb_end, _ = lax.while_loop(lambda st: (st[0] >= 0) & (st[1] == 0), body,
                               (jnp.int32(nblk - 1), jnp.int32(0)))

    @pl.when(jb_end >= 0)
    def _():
        for c in copies(b, jb_end, later_slot(jb_end)):
            c.wait()

    _store_heads(o_ref, acc_ref)


def _attn_sample_call(q, kb, vb, cache_kt, cache_vt, layer, *, tk):
    b, n, w = q.shape
    past = cache_kt.shape[4]
    tk = min(tk, past)
    assert past % tk == 0 and past >= tk
    newspec = pl.BlockSpec((1, n, w), lambda bi: (bi, 0, 0))
    anyspec = pl.BlockSpec(memory_space=pl.ANY)
    return pl.pallas_call(
        functools.partial(_attn_sample_kernel, layer=layer, tq=n, tk=tk, nblk=past // tk),
        grid=(b,),
        in_specs=[newspec, newspec, newspec, anyspec, anyspec],
        out_specs=newspec,
        out_shape=jax.ShapeDtypeStruct((b, n, w), BF16),
        scratch_shapes=[pltpu.VMEM((SB_HEADS, n, SB_HEAD_DIM), F32),
                        pltpu.VMEM((SB_HEADS, n, 128), F32),
                        pltpu.VMEM((SB_HEADS * n, 2 * tk), BF16),
                        pltpu.VMEM((SB_HEADS, n, tk), F32),
                        pltpu.VMEM((4, SB_HEADS, SB_HEAD_DIM, tk), F32),
                        pltpu.VMEM((4, SB_HEADS, SB_HEAD_DIM, tk), F32),
                        pltpu.SemaphoreType.DMA((2, 4))],
        compiler_params=pltpu.CompilerParams(
            dimension_semantics=("arbitrary",), vmem_limit_bytes=VMEM_LIMIT),
        name="attn_sample",
    )(q, kb, vb, cache_kt, cache_vt)


def _outproj_kernel(x_ref, po_ref, at_ref, mod_ref, w_out_ref, g_ref, b_ref, x1_ref, mix_ref, *, bt, tn):
    if bt == 1:
        rc = _pick(tn, ROW_CHUNK)
        g1 = mod_ref[0, 2:3, :]
        for r0 in range(0, tn, rc):
            rows = slice(r0, r0 + rc)
            mix = (_bdot(po_ref[0, rows, :], w_out_ref[0:POOL_WIDTH, :])
                   + _bdot(at_ref[0, rows, :], w_out_ref[POOL_WIDTH:, :]))
            y = ALPHA * x_ref[0, rows, :] + (1.0 + g1) * mix
            x1_ref[0, rows, :] = _norm(y) * g_ref[...] + b_ref[...]
        return
    for s in range(bt):
        rows = slice(s * tn, (s + 1) * tn)
        mix_ref[rows, :] = (_bdot(po_ref[s], w_out_ref[0:POOL_WIDTH, :])
                            + _bdot(at_ref[s], w_out_ref[POOL_WIDTH:, :]))
    for s in range(bt):
        rows = slice(s * tn, (s + 1) * tn)
        g1 = mod_ref[s, 2:3, :]
        y = ALPHA * x_ref[s] + (1.0 + g1) * mix_ref[rows, :]
        x1_ref[s] = _norm(y) * g_ref[...] + b_ref[...]


def _outproj_call(x, po, at, mod, w_out, ln_g, ln_b, *, bt, tn):
    b, n, d = x.shape
    tok = lambda width: pl.BlockSpec((bt, tn, width), lambda bi, j: (bi, j, 0))
    const2 = lambda s: pl.BlockSpec(s, lambda bi, j: (0, 0))
    return pl.pallas_call(
        functools.partial(_outproj_kernel, bt=bt, tn=tn),
        grid=(b // bt, n // tn),
        in_specs=[tok(d), tok(POOL_WIDTH), tok(SB_WIDTH),
                  pl.BlockSpec((bt, 6, d), lambda bi, j: (bi, 0, 0)),
                  const2(w_out.shape), const2((1, d)), const2((1, d))],
        out_specs=tok(d),
        out_shape=jax.ShapeDtypeStruct((b, n, d), F32),
        scratch_shapes=[pltpu.VMEM((bt * tn, d), F32)],
        compiler_params=pltpu.CompilerParams(
            dimension_semantics=("arbitrary", "arbitrary"), vmem_limit_bytes=VMEM_LIMIT),
        name="outproj",
    )(x, po, at, mod, w_out, ln_g, ln_b)


def _route(logits):
    col = lax.broadcasted_iota(jnp.int32, logits.shape, 1)
    colf = col.astype(F32)
    is_group = col < N_GROUPS
    gl = jnp.where(is_group, logits, NEG_BIG)
    gmax = jnp.max(gl, axis=1, keepdims=True)
    gsum = jnp.sum(jnp.where(is_group, jnp.exp(gl - gmax), 0.0), axis=1, keepdims=True)
    pg = 1.0 / gsum
    gidx = jnp.min(jnp.where(gl == gmax, colf, 1e9), axis=1, keepdims=True)
    egrp = ((col - N_GROUPS) >> 2).astype(F32)
    in_group = (col >= N_GROUPS) & (col < N_GROUPS + N_EXPERTS) & (egrp == gidx)
    el = jnp.where(in_group, logits, NEG_BIG)
    m1 = jnp.max(el, axis=1, keepdims=True)
    i1 = jnp.min(jnp.where(el == m1, colf, 1e9), axis=1, keepdims=True)
    el2 = jnp.where(colf == i1, NEG_BIG, el)
    m2 = jnp.max(el2, axis=1, keepdims=True)
    i2 = jnp.min(jnp.where(el2 == m2, colf, 1e9), axis=1, keepdims=True)
    r = jnp.exp(m2 - m1)
    w1 = pg / (1.0 + r)
    w2 = pg * r / (1.0 + r)
    return jnp.where(colf == i1, w1, 0.0) + jnp.where(colf == i2, w2, 0.0)


def _moe_prologue(x, sh2, sc2, wr_ref, br_ref):
    hn = (_norm(x) * (1.0 + sc2) + sh2).astype(BF16)
    return hn, _route(_bdot(hn, wr_ref[0]) + br_ref[0])


def _moe_kernel(x1_ref, mod_ref, wr_ref, br_ref, wup_ref, wdn_ref, g_ref, b_ref, o_ref,
                hn_ref, gate_ref, acc_ref, *, bt, tn):
    g = pl.program_id(2)

    @pl.when(g == 0)
    def _():
        for s in range(bt):
            hn, gate = _moe_prologue(x1_ref[s], mod_ref[s, 3:4, :], mod_ref[s, 4:5, :], wr_ref, br_ref)
            hn_ref[s * tn:(s + 1) * tn, :] = hn
            gate_ref[s * tn:(s + 1) * tn, :] = gate

    hn = hn_ref[...]
    gate = gate_ref[...]
    col = lax.broadcasted_iota(jnp.int32, gate.shape, 1)
    y = None
    for e in range(EXPERTS_PER_GROUP):
        hid = _bdot(hn, wup_ref[0, e])
        a = hid[:, 0:D_EXPERT]
        u = hid[:, D_EXPERT:]
        lane = N_GROUPS + g * EXPERTS_PER_GROUP + e
        gcol = jnp.sum(jnp.where(col == lane, gate, 0.0), axis=1, keepdims=True)
        act = a * (1.0 / (1.0 + jnp.exp(-a))) * u * gcol
        part = _bdot(act.astype(BF16), wdn_ref[0, e])
        y = part if y is None else y + part

    @pl.when(g == 0)
    def _():
        acc_ref[...] = y

    @pl.when(g > 0)
    def _():
        acc_ref[...] += y

    @pl.when(g == pl.num_programs(2) - 1)
    def _():
        for s in range(bt):
            g2 = mod_ref[s, 5:6, :]
            y2 = ALPHA * x1_ref[s] + (1.0 + g2) * acc_ref[s * tn:(s + 1) * tn, :]
            o_ref[s] = _norm(y2) * g_ref[0] + b_ref[0]


def _moe_call(x1, mod, w_r, b_r, w_up, w_down, ln_g, ln_b, layer, *, bt, tn):
    b, n, d = x1.shape
    tm = bt * tn
    tok = pl.BlockSpec((bt, tn, d), lambda bi, j, g: (bi, j, 0))
    per_layer = lambda a: pl.BlockSpec((1,) + a.shape[1:], lambda bi, j, g: (layer,) + (0,) * (a.ndim - 1))
    grp = lambda w: pl.BlockSpec((1, EXPERTS_PER_GROUP) + w.shape[2:], lambda bi, j, g: (layer, g, 0, 0))
    return pl.pallas_call(
        functools.partial(_moe_kernel, bt=bt, tn=tn),
        grid=(b // bt, n // tn, N_GROUPS),
        in_specs=[tok,
                  pl.BlockSpec((bt, 6, d), lambda bi, j, g: (bi, 0, 0)),
                  per_layer(w_r), per_layer(b_r),
                  grp(w_up), grp(w_down),
                  per_layer(ln_g), per_layer(ln_b)],
        out_specs=tok,
        out_shape=jax.ShapeDtypeStruct((b, n, d), F32),
        scratch_shapes=[pltpu.VMEM((tm, d), BF16),
                        pltpu.VMEM((tm, ROUTER_LANES), F32),
                        pltpu.VMEM((tm, d), F32)],
        compiler_params=pltpu.CompilerParams(
            dimension_semantics=("arbitrary", "arbitrary", "arbitrary"), vmem_limit_bytes=VMEM_LIMIT),
        name="moe",
    )(x1, mod, w_r, b_r, w_up, w_down, ln_g, ln_b)


def _pick(n, pref):
    t = min(n, pref)
    while n % t:
        t //= 2
    return t


def kernel(x_prompt, x_sample, c_prompt, c_sample, cache_k, cache_v, state_pool, w_mod, b_mod, w_in, w_pool, pool_scale, w_out, ln1_g, ln1_b, ln2_g, ln2_b, w_group, b_group, w_router, b_router, w_up, w_down):
    depth = w_mod.shape[0]
    bp, seq, d = x_prompt.shape
    bs, dseq, _ = x_sample.shape
    past = cache_k.shape[2]

    w_mod_b = w_mod.astype(BF16)
    w_in_b = w_in.astype(BF16)
    w_pool_b = w_pool.astype(BF16)
    w_out_b = w_out.astype(BF16)
    w_up_b = w_up.astype(BF16)
    w_down_b = w_down.astype(BF16)
    pad = jnp.zeros((depth, d, ROUTER_LANES - N_GROUPS - N_EXPERTS), F32)
    w_r = jnp.concatenate([w_group, w_router, pad], axis=-1).astype(BF16)
    b_r = jnp.concatenate([b_group, b_router, pad[:, 0, :]], axis=-1).reshape(depth, 1, ROUTER_LANES)

    cache_kt = jnp.transpose(cache_k, (0, 1, 3, 4, 2))
    cache_vt = jnp.transpose(cache_v, (0, 1, 3, 4, 2))

    ln2_g3 = ln2_g.reshape(depth, 1, d)
    ln2_b3 = ln2_b.reshape(depth, 1, d)

    mod_all = _mod_call(jnp.concatenate([c_prompt, c_sample], axis=0), w_mod_b, b_mod)
    mod_p = mod_all[:, :bp].reshape(depth, bp, 6, d)
    mod_s = mod_all[:, bp:].reshape(depth, bs, 6, d)

    tm_p = _pick(seq, 1024)
    tq_p = _pick(seq, 512)
    tn_p = _pick(seq, 1024)
    bt_s = _pick(bs, max(1, 1024 // dseq))

    def run_layer(l, x, mod, hist16, pos0, is_prompt):
        q, k, v, kb, vb, po, new_pool = _inproj_call(
            x, mod, hist16, w_in_b[l], w_pool_b[l], pool_scale[l].reshape(1, POOL_WIDTH),
            tm=tm_p if is_prompt else dseq, pos0=pos0)
        if is_prompt:
            at = _attn_prompt_call(q, kb, vb, tq=tq_p, ts=min(tq_p, 128))
            bt, tn = 1, tn_p
        else:
            at = _attn_sample_call(q, kb, vb, cache_kt, cache_vt, l, tk=256)
            bt, tn = bt_s, dseq
        x1 = _outproj_call(x, po, at, mod, w_out_b[l], ln1_g[l].reshape(1, d), ln1_b[l].reshape(1, d),
                           bt=bt, tn=tn)
        x2 = _moe_call(x1, mod, w_r, b_r, w_up_b, w_down_b, ln2_g3, ln2_b3, l, bt=bt, tn=tn)
        b, n = x.shape[0], x.shape[1]
        return (x2, k.reshape(b, n, SB_HEADS, SB_HEAD_DIM), v.reshape(b, n, SB_HEADS, SB_HEAD_DIM),
                new_pool[:, HIST_ROWS - POOL_HIST:])

    y = x_prompt
    kp, vp, pp = [], [], []
    zero_hist = jnp.zeros((bp, HIST_ROWS, POOL_WIDTH), F32)
    for l in range(depth):
        y, k_new, v_new, p_new = run_layer(l, y, mod_p[l], zero_hist, 0, True)
        kp.append(k_new); vp.append(v_new); pp.append(p_new)
    y_prompt = y

    y = x_sample
    kd, vd, pd = [], [], []
    hist_s = jnp.pad(state_pool, ((0, 0), (0, 0), (HIST_ROWS - POOL_HIST, 0), (0, 0)))
    for l in range(depth):
        y, k_new, v_new, p_new = run_layer(l, y, mod_s[l], hist_s[l], past, False)
        kd.append(k_new); vd.append(v_new); pd.append(p_new)
    y_sample = y

    return (y_prompt, y_sample, jnp.stack(kp), jnp.stack(vp), jnp.stack(pp),
            jnp.stack(kd), jnp.stack(vd), jnp.stack(pd))
```

```python
import functools

import jax
import jax.numpy as jnp
from jax import lax
from jax.experimental import pallas as pl
from jax.experimental.pallas import tpu as pltpu

F32 = jnp.float32
BF16 = jnp.bfloat16

D_MODEL = 1024
POOL_WINDOWS = (2, 4, 8, 16)
POOL_WIDTH = 512
POOL_GROUP_DIM = 128
POOL_HIST = 15
HIST_ROWS = 16
SB_WIDTH = 512
SB_HEADS = 8
SB_HEAD_DIM = 64
N_GROUPS = 4
EXPERTS_PER_GROUP = 4
N_EXPERTS = 16
D_EXPERT = 256
ROUTER_LANES = 128
DEPTH = 2
ALPHA = (2 * DEPTH) ** 0.25
LN_EPS = 1e-5
QK_SCALE = 1.0 / 8.0
LOG_UNDERFLOW = -104.0
NEG_BIG = -1e30
VMEM_LIMIT = 56 * 1024 * 1024
ROW_CHUNK = 256
TOP_ROWS = 64


def _norm(x):
    mu = jnp.mean(x, axis=-1, keepdims=True)
    xc = x - mu
    var = jnp.mean(xc * xc, axis=-1, keepdims=True)
    return xc * lax.rsqrt(var + LN_EPS)


def _bdot(a, b):
    return jnp.dot(a, b, preferred_element_type=F32)


def _mod_kernel(c_ref, w_ref, b_ref, o_ref):
    o_ref[0] = _bdot(c_ref[...].astype(BF16), w_ref[0]) + b_ref[0]


def _mod_call(c_all, w_mod, b_mod):
    depth, d, n6 = w_mod.shape
    bc = c_all.shape[0]
    tn = 1536
    return pl.pallas_call(
        _mod_kernel,
        grid=(depth, n6 // tn),
        in_specs=[
            pl.BlockSpec((bc, d), lambda l, j: (0, 0)),
            pl.BlockSpec((1, d, tn), lambda l, j: (l, 0, j)),
            pl.BlockSpec((1, 1, tn), lambda l, j: (l, 0, j)),
        ],
        out_specs=pl.BlockSpec((1, bc, tn), lambda l, j: (l, 0, j)),
        out_shape=jax.ShapeDtypeStruct((depth, bc, n6), F32),
        name="mod",
    )(c_all, w_mod, b_mod.reshape(depth, 1, n6))


def _inproj_kernel(x_ref, mod_ref, hist_ref, w_in_ref, w_pool_ref, ps_ref,
                   q_ref, k_ref, v_ref, kb_ref, vb_ref, po_ref, np_ref, ue_ref, *, tm, rc, pos0):
    j = pl.program_id(1)
    sh1 = mod_ref[0, 0:1, :]
    sc1 = mod_ref[0, 1:2, :]

    @pl.when(j == 0)
    def _():
        ue_ref[0:HIST_ROWS, :] = hist_ref[0]

    for r0 in range(0, tm, rc):
        rows = slice(r0, r0 + rc)
        hn = (_norm(x_ref[0, rows, :]) * (1.0 + sc1) + sh1).astype(BF16)
        proj = _bdot(hn, w_in_ref[...])
        u = proj[:, 0:POOL_WIDTH]
        q = proj[:, POOL_WIDTH:POOL_WIDTH + SB_WIDTH]
        k = proj[:, POOL_WIDTH + SB_WIDTH:POOL_WIDTH + 2 * SB_WIDTH]
        v = proj[:, POOL_WIDTH + 2 * SB_WIDTH:]
        q_ref[0, rows, :] = (q * QK_SCALE).astype(BF16)
        k_ref[0, rows, :] = k
        v_ref[0, rows, :] = v
        kb_ref[0, rows, :] = k.astype(BF16)
        vb_ref[0, rows, :] = v.astype(BF16)
        ue_ref[HIST_ROWS + r0:HIST_ROWS + r0 + rc, :] = u
        pos = pos0 + j * tm + r0 + lax.broadcasted_iota(jnp.int32, (rc, 1), 0)
        outs = []
        for g, w in enumerate(POOL_WINDOWS):
            c0 = g * POOL_GROUP_DIM
            c1 = c0 + POOL_GROUP_DIM
            acc = u[:, c0:c1]
            for dlt in range(1, w):
                acc = acc + ue_ref[HIST_ROWS + r0 - dlt:HIST_ROWS + r0 - dlt + rc, c0:c1]
            cnt = jnp.minimum(w, pos + 1).astype(F32)
            pooled = acc / cnt - u[:, c0:c1]
            mixed = _bdot(pooled.astype(BF16), w_pool_ref[g])
            outs.append(mixed * ps_ref[:, c0:c1])
        po_ref[0, rows, :] = jnp.concatenate(outs, axis=1).astype(BF16)

    tail = ue_ref[tm:tm + HIST_ROWS, :]
    ue_ref[0:HIST_ROWS, :] = tail

    @pl.when(j == pl.num_programs(1) - 1)
    def _():
        np_ref[0] = tail


def _inproj_call(x, mod, hist16, w_in, w_pool, pool_scale, *, tm, pos0):
    b, n, d = x.shape
    nproj = w_in.shape[1]
    tok = lambda width: pl.BlockSpec((1, tm, width), lambda bi, j: (bi, j, 0))
    per_b = lambda rows, width: pl.BlockSpec((1, rows, width), lambda bi, j: (bi, 0, 0))
    const2 = lambda s: pl.BlockSpec(s, lambda bi, j: (0, 0))
    return pl.pallas_call(
        functools.partial(_inproj_kernel, tm=tm, rc=_pick(tm, ROW_CHUNK), pos0=pos0),
        grid=(b, n // tm),
        in_specs=[
            tok(d),
            per_b(6, d),
            per_b(HIST_ROWS, POOL_WIDTH),
            const2((d, nproj)),
            pl.BlockSpec(w_pool.shape, lambda bi, j: (0, 0, 0)),
            const2((1, POOL_WIDTH)),
        ],
        out_specs=[tok(SB_WIDTH), tok(SB_WIDTH), tok(SB_WIDTH), tok(SB_WIDTH), tok(SB_WIDTH),
                   tok(POOL_WIDTH), per_b(HIST_ROWS, POOL_WIDTH)],
        out_shape=[
            jax.ShapeDtypeStruct((b, n, SB_WIDTH), BF16),
            jax.ShapeDtypeStruct((b, n, SB_WIDTH), F32),
            jax.ShapeDtypeStruct((b, n, SB_WIDTH), F32),
            jax.ShapeDtypeStruct((b, n, SB_WIDTH), BF16),
            jax.ShapeDtypeStruct((b, n, SB_WIDTH), BF16),
            jax.ShapeDtypeStruct((b, n, POOL_WIDTH), BF16),
            jax.ShapeDtypeStruct((b, HIST_ROWS, POOL_WIDTH), F32),
        ],
        scratch_shapes=[pltpu.VMEM((tm + HIST_ROWS, POOL_WIDTH), F32)],
        compiler_params=pltpu.CompilerParams(
            dimension_semantics=("arbitrary", "arbitrary"), vmem_limit_bytes=VMEM_LIMIT),
        name="inproj",
    )(x, mod, hist16, w_in, w_pool, pool_scale)


def _suffix_matrix(tk):
    r = lax.broadcasted_iota(jnp.int32, (2 * tk, tk), 0)
    c = lax.broadcasted_iota(jnp.int32, (2 * tk, tk), 1)
    r = jnp.where(r >= tk, r - tk, r)
    return jnp.where(r > c, 1.0, 0.0).astype(BF16)


def _split_hi_lo(x):
    hi = x.astype(BF16)
    return hi, (x - hi.astype(F32)).astype(BF16)


def _sb_chains_batched(chains, acc_ref, carry_ref, hl_ref, lb_ref, mask, m2, first, keys_minor=False):
    tq = acc_ref.shape[1]
    tk = m2.shape[1]
    k_contract = 0 if keys_minor else 1
    tots = []
    for c, (get_q, get_k, _) in enumerate(chains):
        z = lax.dot_general(get_q(), get_k(), (((1,), (k_contract,)), ((), ())),
                            preferred_element_type=F32)
        l1p = jnp.log(1.0 + jnp.exp(-jnp.abs(z)))
        log_beta = jnp.minimum(z, 0.0) - l1p
        log_1m = log_beta - z
        if mask is not None:
            log_1m = jnp.where(mask, log_1m, 0.0)
        hi, lo = _split_hi_lo(log_1m)
        hl_ref[c * tq:(c + 1) * tq, :] = jnp.concatenate([hi, lo], axis=1)
        lb_ref[c] = log_beta
        tots.append(jnp.broadcast_to(jnp.sum(log_1m, axis=1, keepdims=True), (tq, 128)))
    suffix = _bdot(hl_ref[...], m2)
    for c, (_, _, get_v) in enumerate(chains):
        arg = lb_ref[c] + suffix[c * tq:(c + 1) * tq]
        if not first:
            arg = arg + (jnp.concatenate([carry_ref[c]] * (tk // 128), axis=1) if tk > 128 else carry_ref[c])
        a = jnp.exp(arg)
        if mask is not None:
            a = jnp.where(mask, a, 0.0)
        pv = lax.dot_general(a.astype(BF16), get_v(), (((1,), (1 - k_contract,)), ((), ())),
                             preferred_element_type=F32)
        if first:
            acc_ref[c] = pv
            carry_ref[c] = tots[c]
        else:
            acc_ref[c] += pv
            carry_ref[c] += tots[c]


def _head_slices():
    return [slice(h * SB_HEAD_DIM, (h + 1) * SB_HEAD_DIM) for h in range(SB_HEADS)]


def _all_underflowed(carry_ref):
    return (jnp.max(carry_ref[...]) < LOG_UNDERFLOW).astype(jnp.int32)


def _store_heads(o_ref, acc_ref):
    for h in range(SB_HEADS):
        o_ref[0, :, h * SB_HEAD_DIM:(h + 1) * SB_HEAD_DIM] = acc_ref[h].astype(BF16)


def _diag_mask(t):
    row = lax.broadcasted_iota(jnp.int32, (t, t), 0)
    col = lax.broadcasted_iota(jnp.int32, (t, t), 1)
    return col < row


def _attn_prompt_kernel(q_ref, k_ref, v_ref, o_ref, acc_ref, carry_ref, hl_ref, lb_ref, *, tq, ts, top):
    i = pl.program_id(1)
    nsub = tq // ts
    base = i * nsub
    m2 = _suffix_matrix(ts)
    heads = _head_slices()

    def chains(blocks, nrows):
        out = []
        for s in range(nsub):
            k0 = pl.multiple_of(blocks[s] * ts, ts)
            rows = slice(s * ts, s * ts + nrows)
            for sl in heads:
                out.append((lambda rows=rows, sl=sl: q_ref[0, rows, sl],
                            lambda k0=k0, sl=sl: k_ref[0, pl.ds(k0, ts), sl],
                            lambda k0=k0, sl=sl: v_ref[0, pl.ds(k0, ts), sl]))
        return out

    _sb_chains_batched(chains([base + s for s in range(nsub)], ts), acc_ref, carry_ref, hl_ref, lb_ref,
                       _diag_mask(ts), m2, True)

    def walk(k, nrows):
        nchain = nsub * SB_HEADS
        blocks = []
        for s in range(nsub):
            blk = base + s - k
            if s < nsub - 1:
                @pl.when(blk < 0)
                def _(s=s):
                    carry_ref[s * SB_HEADS:(s + 1) * SB_HEADS] = jnp.full((SB_HEADS, ts, 128), NEG_BIG, F32)
            blocks.append(jnp.maximum(blk, 0))
        _sb_chains_batched(chains(blocks, nrows), acc_ref.at[:, 0:nrows, :], carry_ref.at[:, 0:nrows, :],
                           hl_ref.at[0:nchain * nrows, :], lb_ref.at[:, 0:nrows, :], None, m2, False)

    def underflowed(lo, hi):
        return (jnp.max(carry_ref[:, lo:hi, :]) < LOG_UNDERFLOW).astype(jnp.int32)

    def more_keys(k):
        return base + nsub - 1 - k >= 0

    def full_body(st):
        k, _, _ = st
        walk(k, ts)
        return k + 1, underflowed(top, ts), underflowed(0, top)

    k, _, top_done = lax.while_loop(lambda st: more_keys(st[0]) & (st[1] == 0), full_body,
                                    (jnp.int32(1), jnp.int32(0), jnp.int32(0)))

    def top_body(st):
        k, _ = st
        walk(k, top)
        return k + 1, underflowed(0, top)

    lax.while_loop(lambda st: more_keys(st[0]) & (st[1] == 0), top_body, (k, top_done))
    for s in range(nsub):
        for h, sl in enumerate(heads):
            o_ref[0, s * ts:(s + 1) * ts, sl] = acc_ref[s * SB_HEADS + h].astype(BF16)


def _attn_prompt_call(q, kb, vb, *, tq, ts):
    b, n, w = q.shape
    nchain = SB_HEADS * (tq // ts)
    qspec = pl.BlockSpec((1, tq, w), lambda bi, i: (bi, i, 0))
    kvspec = pl.BlockSpec((1, n, w), lambda bi, i: (bi, 0, 0))
    return pl.pallas_call(
        functools.partial(_attn_prompt_kernel, tq=tq, ts=ts, top=TOP_ROWS),
        grid=(b, n // tq),
        in_specs=[qspec, kvspec, kvspec],
        out_specs=qspec,
        out_shape=jax.ShapeDtypeStruct((b, n, w), BF16),
        scratch_shapes=[pltpu.VMEM((nchain, ts, SB_HEAD_DIM), F32),
                        pltpu.VMEM((nchain, ts, 128), F32),
                        pltpu.VMEM((nchain * ts, 2 * ts), BF16),
                        pltpu.VMEM((nchain, ts, ts), F32)],
        compiler_params=pltpu.CompilerParams(
            dimension_semantics=("arbitrary", "arbitrary"), vmem_limit_bytes=VMEM_LIMIT),
        name="attn_prompt",
    )(q, kb, vb)


def _attn_sample_kernel(q_ref, k_ref, v_ref, ck_hbm, cv_hbm, o_ref, acc_ref, carry_ref, hl_ref, lb_ref,
                        kbuf, vbuf, sem, *, layer, tq, tk, nblk):
    b = pl.program_id(0)
    first_slot = b & 1

    def copies(stream, jb, slot):
        cols = pl.ds(jb * tk, tk)
        return (pltpu.make_async_copy(ck_hbm.at[layer, stream, :, :, cols], kbuf.at[slot], sem.at[0, slot]),
                pltpu.make_async_copy(cv_hbm.at[layer, stream, :, :, cols], vbuf.at[slot], sem.at[1, slot]))

    def later_slot(jb):
        return 2 + ((nblk - 1 - jb) & 1)

    @pl.when(b == 0)
    def _():
        for c in copies(b, nblk - 1, first_slot):
            c.start()

    @pl.when(b + 1 < pl.num_programs(0))
    def _():
        for c in copies(b + 1, nblk - 1, 1 - first_slot):
            c.start()

    tn = 128 * pl.cdiv(tq, 128)
    row = lax.broadcasted_iota(jnp.int32, (tq, tn), 0)
    col = lax.broadcasted_iota(jnp.int32, (tq, tn), 1)
    pad = jnp.zeros((tn - tq, SB_HEAD_DIM), BF16)
    padded = (lambda x: jnp.concatenate([x, pad], axis=0)) if tn > tq else (lambda x: x)
    new_chains = [(lambda sl=sl: q_ref[0, :, sl],
                   lambda sl=sl: padded(k_ref[0, :, sl]),
                   lambda sl=sl: padded(v_ref[0, :, sl])) for sl in _head_slices()]
    _sb_chains_batched(new_chains, acc_ref, carry_ref, hl_ref.at[:, 0:2 * tn], lb_ref.at[:, :, 0:tn],
                       col < row, _suffix_matrix(tn), True)
    m2 = _suffix_matrix(tk)

    def body(st):
        jb, _ = st
        slot = jnp.where(jb == nblk - 1, first_slot, later_slot(jb))
        for c in copies(b, jb, slot):
            c.wait()

        @pl.when(jb > 0)
        def _():
            for c in copies(b, jb - 1, later_slot(jb - 1)):
                c.start()

        chains = [(lambda sl=sl: q_ref[0, :, sl],
                   lambda h=h: kbuf[slot, h].astype(BF16),
                   lambda h=h: vbuf[slot, h].astype(BF16)) for h, sl in enumerate(_head_slices())]
        _sb_chains_batched(chains, acc_ref, carry_ref, hl_ref, lb_ref, None, m2, False, keys_minor=True)
        return jb - 1, _all_underflowed(carry_ref)

    jb_end, _ = lax.while_loop(lambda st: (st[0] >= 0) & (st[1] == 0), body,
                               (jnp.int32(nblk - 1), jnp.int32(0)))

    @pl.when(jb_end >= 0)
    def _():
        for c in copies(b, jb_end, later_slot(jb_end)):
            c.wait()

    _store_heads(o_ref, acc_ref)


def _attn_sample_call(q, kb, vb, cache_kt, cache_vt, layer, *, tk):
    b, n, w = q.shape
    past = cache_kt.shape[4]
    tk = min(tk, past)
    assert past % tk == 0 and past >= tk
    newspec = pl.BlockSpec((1, n, w), lambda bi: (bi, 0, 0))
    anyspec = pl.BlockSpec(memory_space=pl.ANY)
    return pl.pallas_call(
        functools.partial(_attn_sample_kernel, layer=layer, tq=n, tk=tk, nblk=past // tk),
        grid=(b,),
        in_specs=[newspec, newspec, newspec, anyspec, anyspec],
        out_specs=newspec,
        out_shape=jax.ShapeDtypeStruct((b, n, w), BF16),
        scratch_shapes=[pltpu.VMEM((SB_HEADS, n, SB_HEAD_DIM), F32),
                        pltpu.VMEM((SB_HEADS, n, 128), F32),
                        pltpu.VMEM((SB_HEADS * n, 2 * tk), BF16),
                        pltpu.VMEM((SB_HEADS, n, tk), F32),
                        pltpu.VMEM((4, SB_HEADS, SB_HEAD_DIM, tk), F32),
                        pltpu.VMEM((4, SB_HEADS, SB_HEAD_DIM, tk), F32),
                        pltpu.SemaphoreType.DMA((2, 4))],
        compiler_params=pltpu.CompilerParams(
            dimension_semantics=("arbitrary",), vmem_limit_bytes=VMEM_LIMIT),
        name="attn_sample",
    )(q, kb, vb, cache_kt, cache_vt)


def _outproj_kernel(x_ref, po_ref, at_ref, mod_ref, w_out_ref, g_ref, b_ref, x1_ref, mix_ref, *, bt, tn):
    if bt == 1:
        rc = _pick(tn, ROW_CHUNK)
        g1 = mod_ref[0, 2:3, :]
        for r0 in range(0, tn, rc):
            rows = slice(r0, r0 + rc)
            mix = (_bdot(po_ref[0, rows, :], w_out_ref[0:POOL_WIDTH, :])
                   + _bdot(at_ref[0, rows, :], w_out_ref[POOL_WIDTH:, :]))
            y = ALPHA * x_ref[0, rows, :] + (1.0 + g1) * mix
            x1_ref[0, rows, :] = _norm(y) * g_ref[...] + b_ref[...]
        return
    for s in range(bt):
        rows = slice(s * tn, (s + 1) * tn)
        mix_ref[rows, :] = (_bdot(po_ref[s], w_out_ref[0:POOL_WIDTH, :])
                            + _bdot(at_ref[s], w_out_ref[POOL_WIDTH:, :]))
    for s in range(bt):
        rows = slice(s * tn, (s + 1) * tn)
        g1 = mod_ref[s, 2:3, :]
        y = ALPHA * x_ref[s] + (1.0 + g1) * mix_ref[rows, :]
        x1_ref[s] = _norm(y) * g_ref[...] + b_ref[...]


def _outproj_call(x, po, at, mod, w_out, ln_g, ln_b, *, bt, tn):
    b, n, d = x.shape
    tok = lambda width: pl.BlockSpec((bt, tn, width), lambda bi, j: (bi, j, 0))
    const2 = lambda s: pl.BlockSpec(s, lambda bi, j: (0, 0))
    return pl.pallas_call(
        functools.partial(_outproj_kernel, bt=bt, tn=tn),
        grid=(b // bt, n // tn),
        in_specs=[tok(d), tok(POOL_WIDTH), tok(SB_WIDTH),
                  pl.BlockSpec((bt, 6, d), lambda bi, j: (bi, 0, 0)),
                  const2(w_out.shape), const2((1, d)), const2((1, d))],
        out_specs=tok(d),
        out_shape=jax.ShapeDtypeStruct((b, n, d), F32),
        scratch_shapes=[pltpu.VMEM((bt * tn, d), F32)],
        compiler_params=pltpu.CompilerParams(
            dimension_semantics=("arbitrary", "arbitrary"), vmem_limit_bytes=VMEM_LIMIT),
        name="outproj",
    )(x, po, at, mod, w_out, ln_g, ln_b)


def _route(logits):
    col = lax.broadcasted_iota(jnp.int32, logits.shape, 1)
    colf = col.astype(F32)
    is_group = col < N_GROUPS
    gl = jnp.where(is_group, logits, NEG_BIG)
    gmax = jnp.max(gl, axis=1, keepdims=True)
    gsum = jnp.sum(jnp.where(is_group, jnp.exp(gl - gmax), 0.0), axis=1, keepdims=True)
    pg = 1.0 / gsum
    gidx = jnp.min(jnp.where(gl == gmax, colf, 1e9), axis=1, keepdims=True)
    egrp = ((col - N_GROUPS) >> 2).astype(F32)
    in_group = (col >= N_GROUPS) & (col < N_GROUPS + N_EXPERTS) & (egrp == gidx)
    el = jnp.where(in_group, logits, NEG_BIG)
    m1 = jnp.max(el, axis=1, keepdims=True)
    i1 = jnp.min(jnp.where(el == m1, colf, 1e9), axis=1, keepdims=True)
    el2 = jnp.where(colf == i1, NEG_BIG, el)
    m2 = jnp.max(el2, axis=1, keepdims=True)
    i2 = jnp.min(jnp.where(el2 == m2, colf, 1e9), axis=1, keepdims=True)
    r = jnp.exp(m2 - m1)
    w1 = pg / (1.0 + r)
    w2 = pg * r / (1.0 + r)
    return jnp.where(colf == i1, w1, 0.0) + jnp.where(colf == i2, w2, 0.0)


def _moe_prologue(x, sh2, sc2, wr_ref, br_ref):
    hn = _norm(x) * (1.0 + sc2) + sh2
    hi, lo = _split_hi_lo(hn)
    w = wr_ref[0]
    p = _bdot(hi, w)
    logits = p[:, 0:ROUTER_LANES] + p[:, ROUTER_LANES:] + _bdot(lo, w[:, 0:ROUTER_LANES]) + br_ref[0]
    return hi, _route(logits)


def _moe_kernel(x1_ref, mod_ref, wr_ref, br_ref, wup_ref, wdn_ref, g_ref, b_ref, o_ref,
                hn_ref, gate_ref, acc_ref, *, bt, tn):
    g = pl.program_id(2)

    @pl.when(g == 0)
    def _():
        for s in range(bt):
            hn, gate = _moe_prologue(x1_ref[s], mod_ref[s, 3:4, :], mod_ref[s, 4:5, :], wr_ref, br_ref)
            hn_ref[s * tn:(s + 1) * tn, :] = hn
            gate_ref[s * tn:(s + 1) * tn, :] = gate

    hn = hn_ref[...]
    gate = gate_ref[...]
    col = lax.broadcasted_iota(jnp.int32, gate.shape, 1)
    y = None
    for e in range(EXPERTS_PER_GROUP):
        hid = _bdot(hn, wup_ref[0, e])
        a = hid[:, 0:D_EXPERT]
        u = hid[:, D_EXPERT:]
        lane = N_GROUPS + g * EXPERTS_PER_GROUP + e
        gcol = jnp.sum(jnp.where(col == lane, gate, 0.0), axis=1, keepdims=True)
        act = a * (1.0 / (1.0 + jnp.exp(-a))) * u * gcol
        part = _bdot(act.astype(BF16), wdn_ref[0, e])
        y = part if y is None else y + part

    @pl.when(g == 0)
    def _():
        acc_ref[...] = y

    @pl.when(g > 0)
    def _():
        acc_ref[...] += y

    @pl.when(g == pl.num_programs(2) - 1)
    def _():
        for s in range(bt):
            g2 = mod_ref[s, 5:6, :]
            y2 = ALPHA * x1_ref[s] + (1.0 + g2) * acc_ref[s * tn:(s + 1) * tn, :]
            o_ref[s] = _norm(y2) * g_ref[0] + b_ref[0]


def _moe_call(x1, mod, w_r, b_r, w_up, w_down, ln_g, ln_b, layer, *, bt, tn):
    b, n, d = x1.shape
    tm = bt * tn
    tok = pl.BlockSpec((bt, tn, d), lambda bi, j, g: (bi, j, 0))
    per_layer = lambda a: pl.BlockSpec((1,) + a.shape[1:], lambda bi, j, g: (layer,) + (0,) * (a.ndim - 1))
    grp = lambda w: pl.BlockSpec((1, EXPERTS_PER_GROUP) + w.shape[2:], lambda bi, j, g: (layer, g, 0, 0))
    return pl.pallas_call(
        functools.partial(_moe_kernel, bt=bt, tn=tn),
        grid=(b // bt, n // tn, N_GROUPS),
        in_specs=[tok,
                  pl.BlockSpec((bt, 6, d), lambda bi, j, g: (bi, 0, 0)),
                  per_layer(w_r), per_layer(b_r),
                  grp(w_up), grp(w_down),
                  per_layer(ln_g), per_layer(ln_b)],
        out_specs=tok,
        out_shape=jax.ShapeDtypeStruct((b, n, d), F32),
        scratch_shapes=[pltpu.VMEM((tm, d), BF16),
                        pltpu.VMEM((tm, ROUTER_LANES), F32),
                        pltpu.VMEM((tm, d), F32)],
        compiler_params=pltpu.CompilerParams(
            dimension_semantics=("arbitrary", "arbitrary", "arbitrary"), vmem_limit_bytes=VMEM_LIMIT),
        name="moe",
    )(x1, mod, w_r, b_r, w_up, w_down, ln_g, ln_b)


def _pick(n, pref):
    t = min(n, pref)
    while n % t:
        t //= 2
    return t


def kernel(x_prompt, x_sample, c_prompt, c_sample, cache_k, cache_v, state_pool, w_mod, b_mod, w_in, w_pool, pool_scale, w_out, ln1_g, ln1_b, ln2_g, ln2_b, w_group, b_group, w_router, b_router, w_up, w_down):
    depth = w_mod.shape[0]
    bp, seq, d = x_prompt.shape
    bs, dseq, _ = x_sample.shape
    past = cache_k.shape[2]

    w_mod_b = w_mod.astype(BF16)
    w_in_b = w_in.astype(BF16)
    w_pool_b = w_pool.astype(BF16)
    w_out_b = w_out.astype(BF16)
    w_up_b = w_up.astype(BF16)
    w_down_b = w_down.astype(BF16)
    pad = jnp.zeros((depth, d, ROUTER_LANES - N_GROUPS - N_EXPERTS), F32)
    w_r32 = jnp.concatenate([w_group, w_router, pad], axis=-1)
    w_r_hi = w_r32.astype(BF16)
    w_r = jnp.concatenate([w_r_hi, (w_r32 - w_r_hi.astype(F32)).astype(BF16)], axis=-1)
    b_r = jnp.concatenate([b_group, b_router, pad[:, 0, :]], axis=-1).reshape(depth, 1, ROUTER_LANES)

    cache_kt = jnp.transpose(cache_k, (0, 1, 3, 4, 2))
    cache_vt = jnp.transpose(cache_v, (0, 1, 3, 4, 2))

    ln2_g3 = ln2_g.reshape(depth, 1, d)
    ln2_b3 = ln2_b.reshape(depth, 1, d)

    mod_all = _mod_call(jnp.concatenate([c_prompt, c_sample], axis=0), w_mod_b, b_mod)
    mod_p = mod_all[:, :bp].reshape(depth, bp, 6, d)
    mod_s = mod_all[:, bp:].reshape(depth, bs, 6, d)

    tm_p = _pick(seq, 1024)
    tq_p = _pick(seq, 512)
    tn_p = _pick(seq, 1024)
    bt_s = _pick(bs, max(1, 1024 // dseq))

    def run_layer(l, x, mod, hist16, pos0, is_prompt):
        q, k, v, kb, vb, po, new_pool = _inproj_call(
            x, mod, hist16, w_in_b[l], w_pool_b[l], pool_scale[l].reshape(1, POOL_WIDTH),
            tm=tm_p if is_prompt else dseq, pos0=pos0)
        if is_prompt:
            at = _attn_prompt_call(q, kb, vb, tq=tq_p, ts=min(tq_p, 128))
            bt, tn = 1, tn_p
        else:
            at = _attn_sample_call(q, kb, vb, cache_kt, cache_vt, l, tk=256)
            bt, tn = bt_s, dseq
        x1 = _outproj_call(x, po, at, mod, w_out_b[l], ln1_g[l].reshape(1, d), ln1_b[l].reshape(1, d),
                           bt=bt, tn=tn)
        x2 = _moe_call(x1, mod, w_r, b_r, w_up_b, w_down_b, ln2_g3, ln2_b3, l, bt=bt, tn=tn)
        b, n = x.shape[0], x.shape[1]
        return (x2, k.reshape(b, n, SB_HEADS, SB_HEAD_DIM), v.reshape(b, n, SB_HEADS, SB_HEAD_DIM),
                new_pool[:, HIST_ROWS - POOL_HIST:])

    y = x_prompt
    kp, vp, pp = [], [], []
    zero_hist = jnp.zeros((bp, HIST_ROWS, POOL_WIDTH), F32)
    for l in range(depth):
        y, k_new, v_new, p_new = run_layer(l, y, mod_p[l], zero_hist, 0, True)
        kp.append(k_new); vp.append(v_new); pp.append(p_new)
    y_prompt = y

    y = x_sample
    kd, vd, pd = [], [], []
    hist_s = jnp.pad(state_pool, ((0, 0), (0, 0), (HIST_ROWS - POOL_HIST, 0), (0, 0)))
    for l in range(depth):
        y, k_new, v_new, p_new = run_layer(l, y, mod_s[l], hist_s[l], past, False)
        kd.append(k_new); vd.append(v_new); pd.append(p_new)
    y_sample = y

    return (y_prompt, y_sample, jnp.stack(kp), jnp.stack(vp), jnp.stack(pp),
            jnp.stack(kd), jnp.stack(vd), jnp.stack(pd))
```

```python
import functools

import jax
import jax.numpy as jnp
from jax import lax
from jax.experimental import pallas as pl
from jax.experimental.pallas import tpu as pltpu

F32 = jnp.float32
BF16 = jnp.bfloat16

D_MODEL = 1024
POOL_WINDOWS = (2, 4, 8, 16)
POOL_WIDTH = 512
POOL_GROUP_DIM = 128
POOL_HIST = 15
HIST_ROWS = 16
SB_WIDTH = 512
SB_HEADS = 8
SB_HEAD_DIM = 64
N_GROUPS = 4
EXPERTS_PER_GROUP = 4
N_EXPERTS = 16
D_EXPERT = 256
ROUTER_LANES = 128
DEPTH = 2
ALPHA = (2 * DEPTH) ** 0.25
LN_EPS = 1e-5
QK_SCALE = 1.0 / 8.0
LOG_UNDERFLOW = -104.0
NEG_BIG = -1e30
VMEM_LIMIT = 56 * 1024 * 1024
INPROJ_CHUNK = 512
OUTPROJ_CHUNK = 256
TOP_ROWS = 64


def _norm(x):
    mu = jnp.mean(x, axis=-1, keepdims=True)
    xc = x - mu
    var = jnp.mean(xc * xc, axis=-1, keepdims=True)
    return xc * lax.rsqrt(var + LN_EPS)


def _bdot(a, b):
    return jnp.dot(a, b, preferred_element_type=F32)


def _mod_kernel(c_ref, w_ref, b_ref, o_ref):
    o_ref[0] = _bdot(c_ref[...].astype(BF16), w_ref[0]) + b_ref[0]


def _mod_call(c_all, w_mod, b_mod):
    depth, d, n6 = w_mod.shape
    bc = c_all.shape[0]
    tn = 1536
    return pl.pallas_call(
        _mod_kernel,
        grid=(depth, n6 // tn),
        in_specs=[
            pl.BlockSpec((bc, d), lambda l, j: (0, 0)),
            pl.BlockSpec((1, d, tn), lambda l, j: (l, 0, j)),
            pl.BlockSpec((1, 1, tn), lambda l, j: (l, 0, j)),
        ],
        out_specs=pl.BlockSpec((1, bc, tn), lambda l, j: (l, 0, j)),
        out_shape=jax.ShapeDtypeStruct((depth, bc, n6), F32),
        name="mod",
    )(c_all, w_mod, b_mod.reshape(depth, 1, n6))


def _pool_rows(ue_ref, base, u, pos0, w_pool_ref, ps_ref):
    nrows = u.shape[0]
    pos = pos0 + lax.broadcasted_iota(jnp.int32, (nrows, 1), 0)
    outs = []
    for g, w in enumerate(POOL_WINDOWS):
        c0 = g * POOL_GROUP_DIM
        c1 = c0 + POOL_GROUP_DIM
        acc = u[:, c0:c1]
        for dlt in range(1, w):
            acc = acc + ue_ref[base - dlt:base - dlt + nrows, c0:c1]
        cnt = jnp.minimum(w, pos + 1).astype(F32)
        pooled = acc / cnt - u[:, c0:c1]
        mixed = _bdot(pooled.astype(BF16), w_pool_ref[g])
        outs.append(mixed * ps_ref[:, c0:c1])
    return jnp.concatenate(outs, axis=1).astype(BF16)


def _inproj_short_kernel(x_ref, mod_ref, hist_ref, w_in_ref, w_pool_ref, ps_ref,
                         q_ref, k_ref, v_ref, kb_ref, vb_ref, po_ref, np_ref, ue_ref, *, bt, tn, pos0):
    hn = jnp.concatenate(
        [(_norm(x_ref[s]) * (1.0 + mod_ref[s, 1:2, :]) + mod_ref[s, 0:1, :]).astype(BF16) for s in range(bt)],
        axis=0)
    proj = _bdot(hn, w_in_ref[...])
    stride = HIST_ROWS + tn
    for s in range(bt):
        p = proj[s * tn:(s + 1) * tn]
        u = p[:, 0:POOL_WIDTH]
        k = p[:, POOL_WIDTH + SB_WIDTH:POOL_WIDTH + 2 * SB_WIDTH]
        v = p[:, POOL_WIDTH + 2 * SB_WIDTH:]
        q_ref[s] = (p[:, POOL_WIDTH:POOL_WIDTH + SB_WIDTH] * QK_SCALE).astype(BF16)
        k_ref[s] = k
        v_ref[s] = v
        kb_ref[s] = k.astype(BF16)
        vb_ref[s] = v.astype(BF16)
        base = s * stride + HIST_ROWS
        ue_ref[s * stride:base, :] = hist_ref[s]
        ue_ref[base:base + tn, :] = u
        po_ref[s] = _pool_rows(ue_ref, base, u, pos0, w_pool_ref, ps_ref)
        np_ref[s] = ue_ref[base + tn - HIST_ROWS:base + tn, :]


def _inproj_short_call(x, mod, hist16, w_in, w_pool, pool_scale, *, bt, pos0):
    b, n, d = x.shape
    nproj = w_in.shape[1]
    blk = lambda rows, width: pl.BlockSpec((bt, rows, width), lambda bi: (bi, 0, 0))
    const2 = lambda s: pl.BlockSpec(s, lambda bi: (0, 0))
    return pl.pallas_call(
        functools.partial(_inproj_short_kernel, bt=bt, tn=n, pos0=pos0),
        grid=(b // bt,),
        in_specs=[blk(n, d), blk(6, d), blk(HIST_ROWS, POOL_WIDTH), const2((d, nproj)),
                  pl.BlockSpec(w_pool.shape, lambda bi: (0, 0, 0)), const2((1, POOL_WIDTH))],
        out_specs=[blk(n, SB_WIDTH)] * 5 + [blk(n, POOL_WIDTH), blk(HIST_ROWS, POOL_WIDTH)],
        out_shape=[
            jax.ShapeDtypeStruct((b, n, SB_WIDTH), BF16),
            jax.ShapeDtypeStruct((b, n, SB_WIDTH), F32),
            jax.ShapeDtypeStruct((b, n, SB_WIDTH), F32),
            jax.ShapeDtypeStruct((b, n, SB_WIDTH), BF16),
            jax.ShapeDtypeStruct((b, n, SB_WIDTH), BF16),
            jax.ShapeDtypeStruct((b, n, POOL_WIDTH), BF16),
            jax.ShapeDtypeStruct((b, HIST_ROWS, POOL_WIDTH), F32),
        ],
        scratch_shapes=[pltpu.VMEM((bt * (HIST_ROWS + n), POOL_WIDTH), F32)],
        compiler_params=pltpu.CompilerParams(
            dimension_semantics=("arbitrary",), vmem_limit_bytes=VMEM_LIMIT),
        name="inproj_short",
    )(x, mod, hist16, w_in, w_pool, pool_scale)


def _inproj_kernel(x_ref, mod_ref, hist_ref, w_in_ref, w_pool_ref, ps_ref,
                   q_ref, k_ref, v_ref, kb_ref, vb_ref, po_ref, np_ref, ue_ref, *, tm, rc, pos0):
    j = pl.program_id(1)
    sh1 = mod_ref[0, 0:1, :]
    sc1 = mod_ref[0, 1:2, :]

    @pl.when(j == 0)
    def _():
        ue_ref[0:HIST_ROWS, :] = hist_ref[0]

    for r0 in range(0, tm, rc):
        rows = slice(r0, r0 + rc)
        hn = (_norm(x_ref[0, rows, :]) * (1.0 + sc1) + sh1).astype(BF16)
        proj = _bdot(hn, w_in_ref[...])
        u = proj[:, 0:POOL_WIDTH]
        q = proj[:, POOL_WIDTH:POOL_WIDTH + SB_WIDTH]
        k = proj[:, POOL_WIDTH + SB_WIDTH:POOL_WIDTH + 2 * SB_WIDTH]
        v = proj[:, POOL_WIDTH + 2 * SB_WIDTH:]
        q_ref[0, rows, :] = (q * QK_SCALE).astype(BF16)
        k_ref[0, rows, :] = k
        v_ref[0, rows, :] = v
        kb_ref[0, rows, :] = k.astype(BF16)
        vb_ref[0, rows, :] = v.astype(BF16)
        ue_ref[HIST_ROWS + r0:HIST_ROWS + r0 + rc, :] = u
        po_ref[0, rows, :] = _pool_rows(ue_ref, HIST_ROWS + r0, u, pos0 + j * tm + r0, w_pool_ref, ps_ref)

    tail = ue_ref[tm:tm + HIST_ROWS, :]
    ue_ref[0:HIST_ROWS, :] = tail

    @pl.when(j == pl.num_programs(1) - 1)
    def _():
        np_ref[0] = tail


def _inproj_call(x, mod, hist16, w_in, w_pool, pool_scale, *, tm, pos0):
    b, n, d = x.shape
    nproj = w_in.shape[1]
    tok = lambda width: pl.BlockSpec((1, tm, width), lambda bi, j: (bi, j, 0))
    per_b = lambda rows, width: pl.BlockSpec((1, rows, width), lambda bi, j: (bi, 0, 0))
    const2 = lambda s: pl.BlockSpec(s, lambda bi, j: (0, 0))
    return pl.pallas_call(
        functools.partial(_inproj_kernel, tm=tm, rc=_pick(tm, INPROJ_CHUNK), pos0=pos0),
        grid=(b, n // tm),
        in_specs=[
            tok(d),
            per_b(6, d),
            per_b(HIST_ROWS, POOL_WIDTH),
            const2((d, nproj)),
            pl.BlockSpec(w_pool.shape, lambda bi, j: (0, 0, 0)),
            const2((1, POOL_WIDTH)),
        ],
        out_specs=[tok(SB_WIDTH), tok(SB_WIDTH), tok(SB_WIDTH), tok(SB_WIDTH), tok(SB_WIDTH),
                   tok(POOL_WIDTH), per_b(HIST_ROWS, POOL_WIDTH)],
        out_shape=[
            jax.ShapeDtypeStruct((b, n, SB_WIDTH), BF16),
            jax.ShapeDtypeStruct((b, n, SB_WIDTH), F32),
            jax.ShapeDtypeStruct((b, n, SB_WIDTH), F32),
            jax.ShapeDtypeStruct((b, n, SB_WIDTH), BF16),
            jax.ShapeDtypeStruct((b, n, SB_WIDTH), BF16),
            jax.ShapeDtypeStruct((b, n, POOL_WIDTH), BF16),
            jax.ShapeDtypeStruct((b, HIST_ROWS, POOL_WIDTH), F32),
        ],
        scratch_shapes=[pltpu.VMEM((tm + HIST_ROWS, POOL_WIDTH), F32)],
        compiler_params=pltpu.CompilerParams(
            dimension_semantics=("arbitrary", "arbitrary"), vmem_limit_bytes=VMEM_LIMIT),
        name="inproj",
    )(x, mod, hist16, w_in, w_pool, pool_scale)


def _suffix_matrix(tk):
    r = lax.broadcasted_iota(jnp.int32, (2 * tk, tk), 0)
    c = lax.broadcasted_iota(jnp.int32, (2 * tk, tk), 1)
    r = jnp.where(r >= tk, r - tk, r)
    return jnp.where(r > c, 1.0, 0.0).astype(BF16)


def _split_hi_lo(x):
    hi = x.astype(BF16)
    return hi, (x - hi.astype(F32)).astype(BF16)


def _sb_chains_batched(chains, acc_ref, carry_ref, hl_ref, lb_ref, mask, m2, first, keys_minor=False):
    tq = acc_ref.shape[1]
    tk = m2.shape[1]
    k_contract = 0 if keys_minor else 1
    tots = []
    for c, (get_q, get_k, _) in enumerate(chains):
        z = lax.dot_general(get_q(), get_k(), (((1,), (k_contract,)), ((), ())),
                            preferred_element_type=F32)
        l1p = jnp.log(1.0 + jnp.exp(-jnp.abs(z)))
        log_beta = jnp.minimum(z, 0.0) - l1p
        log_1m = log_beta - z
        if mask is not None:
            log_1m = jnp.where(mask, log_1m, 0.0)
        hi, lo = _split_hi_lo(log_1m)
        hl_ref[c * tq:(c + 1) * tq, :] = jnp.concatenate([hi, lo], axis=1)
        lb_ref[c] = log_beta
        tots.append(jnp.broadcast_to(jnp.sum(log_1m, axis=1, keepdims=True), (tq, 128)))
    suffix = _bdot(hl_ref[...], m2)
    for c, (_, _, get_v) in enumerate(chains):
        arg = lb_ref[c] + suffix[c * tq:(c + 1) * tq]
        if not first:
            arg = arg + (jnp.concatenate([carry_ref[c]] * (tk // 128), axis=1) if tk > 128 else carry_ref[c])
        a = jnp.exp(arg)
        if mask is not None:
            a = jnp.where(mask, a, 0.0)
        pv = lax.dot_general(a.astype(BF16), get_v(), (((1,), (1 - k_contract,)), ((), ())),
                             preferred_element_type=F32)
        if first:
            acc_ref[c] = pv
            carry_ref[c] = tots[c]
        else:
            acc_ref[c] += pv
            carry_ref[c] += tots[c]


def _head_slices():
    return [slice(h * SB_HEAD_DIM, (h + 1) * SB_HEAD_DIM) for h in range(SB_HEADS)]


def _all_underflowed(carry_ref):
    return (jnp.max(carry_ref[...]) < LOG_UNDERFLOW).astype(jnp.int32)


def _store_heads(o_ref, acc_ref):
    for h in range(SB_HEADS):
        o_ref[0, :, h * SB_HEAD_DIM:(h + 1) * SB_HEAD_DIM] = acc_ref[h].astype(BF16)


def _diag_mask(t):
    row = lax.broadcasted_iota(jnp.int32, (t, t), 0)
    col = lax.broadcasted_iota(jnp.int32, (t, t), 1)
    return col < row


def _attn_prompt_kernel(q_ref, k_ref, v_ref, o_ref, acc_ref, carry_ref, hl_ref, lb_ref, *, tq, ts, top):
    i = pl.program_id(1)
    nsub = tq // ts
    base = i * nsub
    m2 = _suffix_matrix(ts)
    heads = _head_slices()

    def chains(blocks, nrows):
        out = []
        for s in range(nsub):
            k0 = pl.multiple_of(blocks[s] * ts, ts)
            rows = slice(s * ts, s * ts + nrows)
            for sl in heads:
                out.append((lambda rows=rows, sl=sl: q_ref[0, rows, sl],
                            lambda k0=k0, sl=sl: k_ref[0, pl.ds(k0, ts), sl],
                            lambda k0=k0, sl=sl: v_ref[0, pl.ds(k0, ts), sl]))
        return out

    _sb_chains_batched(chains([base + s for s in range(nsub)], ts), acc_ref, carry_ref, hl_ref, lb_ref,
                       _diag_mask(ts), m2, True)

    def walk(k, nrows):
        nchain = nsub * SB_HEADS
        blocks = []
        for s in range(nsub):
            blk = base + s - k
            if s < nsub - 1:
                @pl.when(blk < 0)
                def _(s=s):
                    carry_ref[s * SB_HEADS:(s + 1) * SB_HEADS] = jnp.full((SB_HEADS, ts, 128), NEG_BIG, F32)
            blocks.append(jnp.maximum(blk, 0))
        _sb_chains_batched(chains(blocks, nrows), acc_ref.at[:, 0:nrows, :], carry_ref.at[:, 0:nrows, :],
                           hl_ref.at[0:nchain * nrows, :], lb_ref.at[:, 0:nrows, :], None, m2, False)

    def underflowed(lo, hi):
        return (jnp.max(carry_ref[:, lo:hi, :]) < LOG_UNDERFLOW).astype(jnp.int32)

    def more_keys(k):
        return base + nsub - 1 - k >= 0

    def full_body(st):
        k, _, _ = st
        walk(k, ts)
        return k + 1, underflowed(top, ts), underflowed(0, top)

    k, _, top_done = lax.while_loop(lambda st: more_keys(st[0]) & (st[1] == 0), full_body,
                                    (jnp.int32(1), jnp.int32(0), jnp.int32(0)))

    def top_body(st):
        k, _ = st
        walk(k, top)
        return k + 1, underflowed(0, top)

    lax.while_loop(lambda st: more_keys(st[0]) & (st[1] == 0), top_body, (k, top_done))
    for s in range(nsub):
        for h, sl in enumerate(heads):
            o_ref[0, s * ts:(s + 1) * ts, sl] = acc_ref[s * SB_HEADS + h].astype(BF16)


def _attn_prompt_call(q, kb, vb, *, tq, ts):
    b, n, w = q.shape
    nchain = SB_HEADS * (tq // ts)
    qspec = pl.BlockSpec((1, tq, w), lambda bi, i: (bi, i, 0))
    kvspec = pl.BlockSpec((1, n, w), lambda bi, i: (bi, 0, 0))
    return pl.pallas_call(
        functools.partial(_attn_prompt_kernel, tq=tq, ts=ts, top=TOP_ROWS),
        grid=(b, n // tq),
        in_specs=[qspec, kvspec, kvspec],
        out_specs=qspec,
        out_shape=jax.ShapeDtypeStruct((b, n, w), BF16),
        scratch_shapes=[pltpu.VMEM((nchain, ts, SB_HEAD_DIM), F32),
                        pltpu.VMEM((nchain, ts, 128), F32),
                        pltpu.VMEM((nchain * ts, 2 * ts), BF16),
                        pltpu.VMEM((nchain, ts, ts), F32)],
        compiler_params=pltpu.CompilerParams(
            dimension_semantics=("arbitrary", "arbitrary"), vmem_limit_bytes=VMEM_LIMIT),
        name="attn_prompt",
    )(q, kb, vb)


def _attn_sample_kernel(q_ref, k_ref, v_ref, ck_hbm, cv_hbm, o_ref, acc_ref, carry_ref, hl_ref, lb_ref,
                        kbuf, vbuf, sem, *, layer, tq, tk, nblk):
    b = pl.program_id(0)
    first_slot = b & 1

    def copies(stream, jb, slot):
        cols = pl.ds(jb * tk, tk)
        return (pltpu.make_async_copy(ck_hbm.at[layer, stream, :, :, cols], kbuf.at[slot], sem.at[0, slot]),
                pltpu.make_async_copy(cv_hbm.at[layer, stream, :, :, cols], vbuf.at[slot], sem.at[1, slot]))

    def later_slot(jb):
        return 2 + ((nblk - 1 - jb) & 1)

    @pl.when(b == 0)
    def _():
        for c in copies(b, nblk - 1, first_slot):
            c.start()

    @pl.when(b + 1 < pl.num_programs(0))
    def _():
        for c in copies(b + 1, nblk - 1, 1 - first_slot):
            c.start()

    tn = 128 * pl.cdiv(tq, 128)
    row = lax.broadcasted_iota(jnp.int32, (tq, tn), 0)
    col = lax.broadcasted_iota(jnp.int32, (tq, tn), 1)
    pad = jnp.zeros((tn - tq, SB_HEAD_DIM), BF16)
    padded = (lambda x: jnp.concatenate([x, pad], axis=0)) if tn > tq else (lambda x: x)
    new_chains = [(lambda sl=sl: q_ref[0, :, sl],
                   lambda sl=sl: padded(k_ref[0, :, sl]),
                   lambda sl=sl: padded(v_ref[0, :, sl])) for sl in _head_slices()]
    _sb_chains_batched(new_chains, acc_ref, carry_ref, hl_ref.at[:, 0:2 * tn], lb_ref.at[:, :, 0:tn],
                       col < row, _suffix_matrix(tn), True)
    m2 = _suffix_matrix(tk)

    def body(st):
        jb, _ = st
        slot = jnp.where(jb == nblk - 1, first_slot, later_slot(jb))
        for c in copies(b, jb, slot):
            c.wait()

        @pl.when(jb > 0)
        def _():
            for c in copies(b, jb - 1, later_slot(jb - 1)):
                c.start()

        chains = [(lambda sl=sl: q_ref[0, :, sl],
                   lambda h=h: kbuf[slot, h].astype(BF16),
                   lambda h=h: vbuf[slot, h].astype(BF16)) for h, sl in enumerate(_head_slices())]
        _sb_chains_batched(chains, acc_ref, carry_ref, hl_ref, lb_ref, None, m2, False, keys_minor=True)
        return jb - 1, _all_underflowed(carry_ref)

    jb_end, _ = lax.while_loop(lambda st: (st[0] >= 0) & (st[1] == 0), body,
                               (jnp.int32(nblk - 1), jnp.int32(0)))

    @pl.when(jb_end >= 0)
    def _():
        for c in copies(b, jb_end, later_slot(jb_end)):
            c.wait()

    _store_heads(o_ref, acc_ref)


def _attn_sample_call(q, kb, vb, cache_kt, cache_vt, layer, *, tk):
    b, n, w = q.shape
    past = cache_kt.shape[4]
    tk = min(tk, past)
    assert past % tk == 0 and past >= tk
    newspec = pl.BlockSpec((1, n, w), lambda bi: (bi, 0, 0))
    anyspec = pl.BlockSpec(memory_space=pl.ANY)
    return pl.pallas_call(
        functools.partial(_attn_sample_kernel, layer=layer, tq=n, tk=tk, nblk=past // tk),
        grid=(b,),
        in_specs=[newspec, newspec, newspec, anyspec, anyspec],
        out_specs=newspec,
        out_shape=jax.ShapeDtypeStruct((b, n, w), BF16),
        scratch_shapes=[pltpu.VMEM((SB_HEADS, n, SB_HEAD_DIM), F32),
                        pltpu.VMEM((SB_HEADS, n, 128), F32),
                        pltpu.VMEM((SB_HEADS * n, 2 * tk), BF16),
                        pltpu.VMEM((SB_HEADS, n, tk), F32),
                        pltpu.VMEM((4, SB_HEADS, SB_HEAD_DIM, tk), F32),
                        pltpu.VMEM((4, SB_HEADS, SB_HEAD_DIM, tk), F32),
                        pltpu.SemaphoreType.DMA((2, 4))],
        compiler_params=pltpu.CompilerParams(
            dimension_semantics=("arbitrary",), vmem_limit_bytes=VMEM_LIMIT),
        name="attn_sample",
    )(q, kb, vb, cache_kt, cache_vt)


def _outproj_kernel(x_ref, po_ref, at_ref, mod_ref, w_out_ref, g_ref, b_ref, x1_ref, mix_ref, *, bt, tn):
    if bt == 1:
        rc = _pick(tn, OUTPROJ_CHUNK)
        g1 = mod_ref[0, 2:3, :]
        for r0 in range(0, tn, rc):
            rows = slice(r0, r0 + rc)
            mix = (_bdot(po_ref[0, rows, :], w_out_ref[0:POOL_WIDTH, :])
                   + _bdot(at_ref[0, rows, :], w_out_ref[POOL_WIDTH:, :]))
            y = ALPHA * x_ref[0, rows, :] + (1.0 + g1) * mix
            x1_ref[0, rows, :] = _norm(y) * g_ref[...] + b_ref[...]
        return
    mix_ref[...] = (_bdot(po_ref[...].reshape(bt * tn, POOL_WIDTH), w_out_ref[0:POOL_WIDTH, :])
                    + _bdot(at_ref[...].reshape(bt * tn, SB_WIDTH), w_out_ref[POOL_WIDTH:, :]))
    for s in range(bt):
        rows = slice(s * tn, (s + 1) * tn)
        g1 = mod_ref[s, 2:3, :]
        y = ALPHA * x_ref[s] + (1.0 + g1) * mix_ref[rows, :]
        x1_ref[s] = _norm(y) * g_ref[...] + b_ref[...]


def _outproj_call(x, po, at, mod, w_out, ln_g, ln_b, *, bt, tn):
    b, n, d = x.shape
    tok = lambda width: pl.BlockSpec((bt, tn, width), lambda bi, j: (bi, j, 0))
    const2 = lambda s: pl.BlockSpec(s, lambda bi, j: (0, 0))
    return pl.pallas_call(
        functools.partial(_outproj_kernel, bt=bt, tn=tn),
        grid=(b // bt, n // tn),
        in_specs=[tok(d), tok(POOL_WIDTH), tok(SB_WIDTH),
                  pl.BlockSpec((bt, 6, d), lambda bi, j: (bi, 0, 0)),
                  const2(w_out.shape), const2((1, d)), const2((1, d))],
        out_specs=tok(d),
        out_shape=jax.ShapeDtypeStruct((b, n, d), F32),
        scratch_shapes=[pltpu.VMEM((bt * tn, d), F32)],
        compiler_params=pltpu.CompilerParams(
            dimension_semantics=("arbitrary", "arbitrary"), vmem_limit_bytes=VMEM_LIMIT),
        name="outproj",
    )(x, po, at, mod, w_out, ln_g, ln_b)


def _route(logits):
    col = lax.broadcasted_iota(jnp.int32, logits.shape, 1)
    colf = col.astype(F32)
    is_group = col < N_GROUPS
    gl = jnp.where(is_group, logits, NEG_BIG)
    gmax = jnp.max(gl, axis=1, keepdims=True)
    gsum = jnp.sum(jnp.where(is_group, jnp.exp(gl - gmax), 0.0), axis=1, keepdims=True)
    pg = 1.0 / gsum
    gidx = jnp.min(jnp.where(gl == gmax, colf, 1e9), axis=1, keepdims=True)
    egrp = ((col - N_GROUPS) >> 2).astype(F32)
    in_group = (col >= N_GROUPS) & (col < N_GROUPS + N_EXPERTS) & (egrp == gidx)
    el = jnp.where(in_group, logits, NEG_BIG)
    m1 = jnp.max(el, axis=1, keepdims=True)
    i1 = jnp.min(jnp.where(el == m1, colf, 1e9), axis=1, keepdims=True)
    el2 = jnp.where(colf == i1, NEG_BIG, el)
    m2 = jnp.max(el2, axis=1, keepdims=True)
    i2 = jnp.min(jnp.where(el2 == m2, colf, 1e9), axis=1, keepdims=True)
    r = jnp.exp(m2 - m1)
    w1 = pg / (1.0 + r)
    w2 = pg * r / (1.0 + r)
    return jnp.where(colf == i1, w1, 0.0) + jnp.where(colf == i2, w2, 0.0)


def _moe_prologue(x, sh2, sc2, wr_ref, br_ref):
    hn = _norm(x) * (1.0 + sc2) + sh2
    hi, lo = _split_hi_lo(hn)
    w = wr_ref[0]
    p = _bdot(hi, w)
    logits = p[:, 0:ROUTER_LANES] + p[:, ROUTER_LANES:] + _bdot(lo, w[:, 0:ROUTER_LANES]) + br_ref[0]
    return hi, _route(logits)


def _moe_kernel(x1_ref, mod_ref, wr_ref, br_ref, wup_ref, wdn_ref, g_ref, b_ref, o_ref,
                hn_ref, gate_ref, acc_ref, *, bt, tn):
    g = pl.program_id(2)

    @pl.when(g == 0)
    def _():
        for s in range(bt):
            hn, gate = _moe_prologue(x1_ref[s], mod_ref[s, 3:4, :], mod_ref[s, 4:5, :], wr_ref, br_ref)
            hn_ref[s * tn:(s + 1) * tn, :] = hn
            gate_ref[s * tn:(s + 1) * tn, :] = gate

    hn = hn_ref[...]
    gate = gate_ref[...]
    col = lax.broadcasted_iota(jnp.int32, gate.shape, 1)
    y = None
    for e in range(EXPERTS_PER_GROUP):
        hid = _bdot(hn, wup_ref[0, e])
        a = hid[:, 0:D_EXPERT]
        u = hid[:, D_EXPERT:]
        lane = N_GROUPS + g * EXPERTS_PER_GROUP + e
        gcol = jnp.sum(jnp.where(col == lane, gate, 0.0), axis=1, keepdims=True)
        act = a * (1.0 / (1.0 + jnp.exp(-a))) * u * gcol
        part = _bdot(act.astype(BF16), wdn_ref[0, e])
        y = part if y is None else y + part

    @pl.when(g == 0)
    def _():
        acc_ref[...] = y

    @pl.when(g > 0)
    def _():
        acc_ref[...] += y

    @pl.when(g == pl.num_programs(2) - 1)
    def _():
        for s in range(bt):
            g2 = mod_ref[s, 5:6, :]
            y2 = ALPHA * x1_ref[s] + (1.0 + g2) * acc_ref[s * tn:(s + 1) * tn, :]
            o_ref[s] = _norm(y2) * g_ref[0] + b_ref[0]


def _moe_call(x1, mod, w_r, b_r, w_up, w_down, ln_g, ln_b, layer, *, bt, tn):
    b, n, d = x1.shape
    tm = bt * tn
    tok = pl.BlockSpec((bt, tn, d), lambda bi, j, g: (bi, j, 0))
    per_layer = lambda a: pl.BlockSpec((1,) + a.shape[1:], lambda bi, j, g: (layer,) + (0,) * (a.ndim - 1))
    grp = lambda w: pl.BlockSpec((1, EXPERTS_PER_GROUP) + w.shape[2:], lambda bi, j, g: (layer, g, 0, 0))
    return pl.pallas_call(
        functools.partial(_moe_kernel, bt=bt, tn=tn),
        grid=(b // bt, n // tn, N_GROUPS),
        in_specs=[tok,
                  pl.BlockSpec((bt, 6, d), lambda bi, j, g: (bi, 0, 0)),
                  per_layer(w_r), per_layer(b_r),
                  grp(w_up), grp(w_down),
                  per_layer(ln_g), per_layer(ln_b)],
        out_specs=tok,
        out_shape=jax.ShapeDtypeStruct((b, n, d), F32),
        scratch_shapes=[pltpu.VMEM((tm, d), BF16),
                        pltpu.VMEM((tm, ROUTER_LANES), F32),
                        pltpu.VMEM((tm, d), F32)],
        compiler_params=pltpu.CompilerParams(
            dimension_semantics=("arbitrary", "arbitrary", "arbitrary"), vmem_limit_bytes=VMEM_LIMIT),
        name="moe",
    )(x1, mod, w_r, b_r, w_up, w_down, ln_g, ln_b)


def _pick(n, pref):
    t = min(n, pref)
    while n % t:
        t //= 2
    return t


def kernel(x_prompt, x_sample, c_prompt, c_sample, cache_k, cache_v, state_pool, w_mod, b_mod, w_in, w_pool, pool_scale, w_out, ln1_g, ln1_b, ln2_g, ln2_b, w_group, b_group, w_router, b_router, w_up, w_down):
    depth = w_mod.shape[0]
    bp, seq, d = x_prompt.shape
    bs, dseq, _ = x_sample.shape
    past = cache_k.shape[2]

    w_mod_b = w_mod.astype(BF16)
    w_in_b = w_in.astype(BF16)
    w_pool_b = w_pool.astype(BF16)
    w_out_b = w_out.astype(BF16)
    w_up_b = w_up.astype(BF16)
    w_down_b = w_down.astype(BF16)
    pad = jnp.zeros((depth, d, ROUTER_LANES - N_GROUPS - N_EXPERTS), F32)
    w_r32 = jnp.concatenate([w_group, w_router, pad], axis=-1)
    w_r_hi = w_r32.astype(BF16)
    w_r = jnp.concatenate([w_r_hi, (w_r32 - w_r_hi.astype(F32)).astype(BF16)], axis=-1)
    b_r = jnp.concatenate([b_group, b_router, pad[:, 0, :]], axis=-1).reshape(depth, 1, ROUTER_LANES)

    cache_kt = jnp.transpose(cache_k, (0, 1, 3, 4, 2))
    cache_vt = jnp.transpose(cache_v, (0, 1, 3, 4, 2))

    ln2_g3 = ln2_g.reshape(depth, 1, d)
    ln2_b3 = ln2_b.reshape(depth, 1, d)

    mod_all = _mod_call(jnp.concatenate([c_prompt, c_sample], axis=0), w_mod_b, b_mod)
    mod_p = mod_all[:, :bp].reshape(depth, bp, 6, d)
    mod_s = mod_all[:, bp:].reshape(depth, bs, 6, d)

    tm_p = _pick(seq, 1024)
    tq_p = _pick(seq, 512)
    tn_p = _pick(seq, 1024)
    bt_s = _pick(bs, max(1, 1024 // dseq))

    def run_layer(l, x, mod, hist16, pos0, is_prompt):
        inproj_args = (x, mod, hist16, w_in_b[l], w_pool_b[l], pool_scale[l].reshape(1, POOL_WIDTH))
        if is_prompt:
            q, k, v, kb, vb, po, new_pool = _inproj_call(*inproj_args, tm=tm_p, pos0=pos0)
        else:
            q, k, v, kb, vb, po, new_pool = _inproj_short_call(
                *inproj_args, bt=_pick(bs, max(1, INPROJ_CHUNK // dseq)), pos0=pos0)
        if is_prompt:
            at = _attn_prompt_call(q, kb, vb, tq=tq_p, ts=min(tq_p, 128))
            bt, tn = 1, tn_p
        else:
            at = _attn_sample_call(q, kb, vb, cache_kt, cache_vt, l, tk=256)
            bt, tn = bt_s, dseq
        x1 = _outproj_call(x, po, at, mod, w_out_b[l], ln1_g[l].reshape(1, d), ln1_b[l].reshape(1, d),
                           bt=bt, tn=tn)
        x2 = _moe_call(x1, mod, w_r, b_r, w_up_b, w_down_b, ln2_g3, ln2_b3, l, bt=bt, tn=tn)
        b, n = x.shape[0], x.shape[1]
        return (x2, k.reshape(b, n, SB_HEADS, SB_HEAD_DIM), v.reshape(b, n, SB_HEADS, SB_HEAD_DIM),
                new_pool[:, HIST_ROWS - POOL_HIST:])

    y = x_prompt
    kp, vp, pp = [], [], []
    zero_hist = jnp.zeros((bp, HIST_ROWS, POOL_WIDTH), F32)
    for l in range(depth):
        y, k_new, v_new, p_new = run_layer(l, y, mod_p[l], zero_hist, 0, True)
        kp.append(k_new); vp.append(v_new); pp.append(p_new)
    y_prompt = y

    y = x_sample
    kd, vd, pd = [], [], []
    hist_s = jnp.pad(state_pool, ((0, 0), (0, 0), (HIST_ROWS - POOL_HIST, 0), (0, 0)))
    for l in range(depth):
        y, k_new, v_new, p_new = run_layer(l, y, mod_s[l], hist_s[l], past, False)
        kd.append(k_new); vd.append(v_new); pd.append(p_new)
    y_sample = y

    return (y_prompt, y_sample, jnp.stack(kp), jnp.stack(vp), jnp.stack(pp),
            jnp.stack(kd), jnp.stack(vd), jnp.stack(pd))
```

```python
import functools

import jax
import jax.numpy as jnp
from jax import lax
from jax.experimental import pallas as pl
from jax.experimental.pallas import tpu as pltpu

F32 = jnp.float32
BF16 = jnp.bfloat16

D_MODEL = 1024
POOL_WINDOWS = (2, 4, 8, 16)
POOL_WIDTH = 512
POOL_GROUP_DIM = 128
POOL_HIST = 15
HIST_ROWS = 16
SB_WIDTH = 512
SB_HEADS = 8
SB_HEAD_DIM = 64
N_GROUPS = 4
EXPERTS_PER_GROUP = 4
N_EXPERTS = 16
D_EXPERT = 256
ROUTER_LANES = 128
DEPTH = 2
ALPHA = (2 * DEPTH) ** 0.25
LN_EPS = 1e-5
QK_SCALE = 1.0 / 8.0
LOG_UNDERFLOW = -104.0
NEG_BIG = -1e30
VMEM_LIMIT = 56 * 1024 * 1024
INPROJ_CHUNK = 512
OUTPROJ_CHUNK = 256
TOP_ROWS = 64


def _norm(x):
    mu = jnp.mean(x, axis=-1, keepdims=True)
    xc = x - mu
    var = jnp.mean(xc * xc, axis=-1, keepdims=True)
    return xc * lax.rsqrt(var + LN_EPS)


def _bdot(a, b):
    return jnp.dot(a, b, preferred_element_type=F32)


def _mod_kernel(c_ref, w_ref, b_ref, o_ref):
    o_ref[0] = _bdot(c_ref[...].astype(BF16), w_ref[0]) + b_ref[0]


def _mod_call(c_all, w_mod, b_mod):
    depth, d, n6 = w_mod.shape
    bc = c_all.shape[0]
    tn = 1536
    return pl.pallas_call(
        _mod_kernel,
        grid=(depth, n6 // tn),
        in_specs=[
            pl.BlockSpec((bc, d), lambda l, j: (0, 0)),
            pl.BlockSpec((1, d, tn), lambda l, j: (l, 0, j)),
            pl.BlockSpec((1, 1, tn), lambda l, j: (l, 0, j)),
        ],
        out_specs=pl.BlockSpec((1, bc, tn), lambda l, j: (l, 0, j)),
        out_shape=jax.ShapeDtypeStruct((depth, bc, n6), F32),
        name="mod",
    )(c_all, w_mod, b_mod.reshape(depth, 1, n6))


def _pool_rows(ue_ref, base, u, pos0, w_pool_ref, ps_ref):
    nrows = u.shape[0]
    pos = pos0 + lax.broadcasted_iota(jnp.int32, (nrows, 1), 0)
    outs = []
    for g, w in enumerate(POOL_WINDOWS):
        c0 = g * POOL_GROUP_DIM
        c1 = c0 + POOL_GROUP_DIM
        acc = u[:, c0:c1]
        for dlt in range(1, w):
            acc = acc + ue_ref[base - dlt:base - dlt + nrows, c0:c1]
        cnt = jnp.minimum(w, pos + 1).astype(F32)
        pooled = acc / cnt - u[:, c0:c1]
        mixed = _bdot(pooled.astype(BF16), w_pool_ref[g])
        outs.append(mixed * ps_ref[:, c0:c1])
    return jnp.concatenate(outs, axis=1).astype(BF16)


def _inproj_short_kernel(x_ref, mod_ref, hist_ref, w_in_ref, w_pool_ref, ps_ref,
                         q_ref, k_ref, v_ref, kb_ref, vb_ref, po_ref, np_ref, ue_ref, *, bt, tn, pos0):
    hn = jnp.concatenate(
        [(_norm(x_ref[s]) * (1.0 + mod_ref[s, 1:2, :]) + mod_ref[s, 0:1, :]).astype(BF16) for s in range(bt)],
        axis=0)
    proj = _bdot(hn, w_in_ref[...])
    stride = HIST_ROWS + tn
    for s in range(bt):
        p = proj[s * tn:(s + 1) * tn]
        u = p[:, 0:POOL_WIDTH]
        k = p[:, POOL_WIDTH + SB_WIDTH:POOL_WIDTH + 2 * SB_WIDTH]
        v = p[:, POOL_WIDTH + 2 * SB_WIDTH:]
        q_ref[s] = (p[:, POOL_WIDTH:POOL_WIDTH + SB_WIDTH] * QK_SCALE).astype(BF16)
        k_ref[s] = k
        v_ref[s] = v
        kb_ref[s] = k.astype(BF16)
        vb_ref[s] = v.astype(BF16)
        base = s * stride + HIST_ROWS
        ue_ref[s * stride:base, :] = hist_ref[s]
        ue_ref[base:base + tn, :] = u
        po_ref[s] = _pool_rows(ue_ref, base, u, pos0, w_pool_ref, ps_ref)
        np_ref[s] = ue_ref[base + tn - HIST_ROWS:base + tn, :]


def _inproj_short_call(x, mod, hist16, w_in, w_pool, pool_scale, *, bt, pos0):
    b, n, d = x.shape
    nproj = w_in.shape[1]
    blk = lambda rows, width: pl.BlockSpec((bt, rows, width), lambda bi: (bi, 0, 0))
    const2 = lambda s: pl.BlockSpec(s, lambda bi: (0, 0))
    return pl.pallas_call(
        functools.partial(_inproj_short_kernel, bt=bt, tn=n, pos0=pos0),
        grid=(b // bt,),
        in_specs=[blk(n, d), blk(6, d), blk(HIST_ROWS, POOL_WIDTH), const2((d, nproj)),
                  pl.BlockSpec(w_pool.shape, lambda bi: (0, 0, 0)), const2((1, POOL_WIDTH))],
        out_specs=[blk(n, SB_WIDTH)] * 5 + [blk(n, POOL_WIDTH), blk(HIST_ROWS, POOL_WIDTH)],
        out_shape=[
            jax.ShapeDtypeStruct((b, n, SB_WIDTH), BF16),
            jax.ShapeDtypeStruct((b, n, SB_WIDTH), F32),
            jax.ShapeDtypeStruct((b, n, SB_WIDTH), F32),
            jax.ShapeDtypeStruct((b, n, SB_WIDTH), BF16),
            jax.ShapeDtypeStruct((b, n, SB_WIDTH), BF16),
            jax.ShapeDtypeStruct((b, n, POOL_WIDTH), BF16),
            jax.ShapeDtypeStruct((b, HIST_ROWS, POOL_WIDTH), F32),
        ],
        scratch_shapes=[pltpu.VMEM((bt * (HIST_ROWS + n), POOL_WIDTH), F32)],
        compiler_params=pltpu.CompilerParams(
            dimension_semantics=("arbitrary",), vmem_limit_bytes=VMEM_LIMIT),
        name="inproj_short",
    )(x, mod, hist16, w_in, w_pool, pool_scale)


def _inproj_kernel(x_ref, mod_ref, hist_ref, w_in_ref, w_pool_ref, ps_ref, *refs, tm, rc, pos0, nprev):
    if nprev:
        kprev_ref, vprev_ref, q_ref, kst_ref, vst_ref, kb_ref, vb_ref, po_ref, np_ref, ue_ref = refs
        kst_ref[0:nprev] = kprev_ref[...]
        vst_ref[0:nprev] = vprev_ref[...]
        k_ref, v_ref = kst_ref.at[nprev], vst_ref.at[nprev]
    else:
        q_ref, k_ref, v_ref, kb_ref, vb_ref, po_ref, np_ref, ue_ref = refs
    j = pl.program_id(1)
    sh1 = mod_ref[0, 0:1, :]
    sc1 = mod_ref[0, 1:2, :]

    @pl.when(j == 0)
    def _():
        ue_ref[0:HIST_ROWS, :] = hist_ref[0]

    for r0 in range(0, tm, rc):
        rows = slice(r0, r0 + rc)
        hn = (_norm(x_ref[0, rows, :]) * (1.0 + sc1) + sh1).astype(BF16)
        proj = _bdot(hn, w_in_ref[...])
        u = proj[:, 0:POOL_WIDTH]
        q = proj[:, POOL_WIDTH:POOL_WIDTH + SB_WIDTH]
        k = proj[:, POOL_WIDTH + SB_WIDTH:POOL_WIDTH + 2 * SB_WIDTH]
        v = proj[:, POOL_WIDTH + 2 * SB_WIDTH:]
        q_ref[0, rows, :] = (q * QK_SCALE).astype(BF16)
        k_ref[0, rows, :] = k
        v_ref[0, rows, :] = v
        kb_ref[0, rows, :] = k.astype(BF16)
        vb_ref[0, rows, :] = v.astype(BF16)
        ue_ref[HIST_ROWS + r0:HIST_ROWS + r0 + rc, :] = u
        po_ref[0, rows, :] = _pool_rows(ue_ref, HIST_ROWS + r0, u, pos0 + j * tm + r0, w_pool_ref, ps_ref)

    tail = ue_ref[tm:tm + HIST_ROWS, :]
    ue_ref[0:HIST_ROWS, :] = tail

    @pl.when(j == pl.num_programs(1) - 1)
    def _():
        np_ref[0] = tail


def _inproj_call(x, mod, hist16, w_in, w_pool, pool_scale, kv_prev=None, *, tm, pos0):
    b, n, d = x.shape
    nproj = w_in.shape[1]
    tok = lambda width: pl.BlockSpec((1, tm, width), lambda bi, j: (bi, j, 0))
    per_b = lambda rows, width: pl.BlockSpec((1, rows, width), lambda bi, j: (bi, 0, 0))
    const2 = lambda s: pl.BlockSpec(s, lambda bi, j: (0, 0))
    prev = [a.reshape((-1, b, n, SB_WIDTH)) for a in (kv_prev or ())]
    nprev = prev[0].shape[0] if prev else 0
    stack = lambda layers: pl.BlockSpec((layers, 1, tm, SB_WIDTH), lambda bi, j: (0, bi, j, 0))
    kv_spec = stack(nprev + 1) if nprev else tok(SB_WIDTH)
    kv_shape = (nprev + 1, b, n, SB_WIDTH) if nprev else (b, n, SB_WIDTH)
    return pl.pallas_call(
        functools.partial(_inproj_kernel, tm=tm, rc=_pick(tm, INPROJ_CHUNK), pos0=pos0, nprev=nprev),
        grid=(b, n // tm),
        in_specs=[
            tok(d),
            per_b(6, d),
            per_b(HIST_ROWS, POOL_WIDTH),
            const2((d, nproj)),
            pl.BlockSpec(w_pool.shape, lambda bi, j: (0, 0, 0)),
            const2((1, POOL_WIDTH)),
        ] + [stack(nprev)] * len(prev),
        out_specs=[tok(SB_WIDTH), kv_spec, kv_spec, tok(SB_WIDTH), tok(SB_WIDTH),
                   tok(POOL_WIDTH), per_b(HIST_ROWS, POOL_WIDTH)],
        out_shape=[
            jax.ShapeDtypeStruct((b, n, SB_WIDTH), BF16),
            jax.ShapeDtypeStruct(kv_shape, F32),
            jax.ShapeDtypeStruct(kv_shape, F32),
            jax.ShapeDtypeStruct((b, n, SB_WIDTH), BF16),
            jax.ShapeDtypeStruct((b, n, SB_WIDTH), BF16),
            jax.ShapeDtypeStruct((b, n, POOL_WIDTH), BF16),
            jax.ShapeDtypeStruct((b, HIST_ROWS, POOL_WIDTH), F32),
        ],
        scratch_shapes=[pltpu.VMEM((tm + HIST_ROWS, POOL_WIDTH), F32)],
        compiler_params=pltpu.CompilerParams(
            dimension_semantics=("arbitrary", "arbitrary"), vmem_limit_bytes=VMEM_LIMIT),
        name="inproj",
    )(x, mod, hist16, w_in, w_pool, pool_scale, *prev)


def _suffix_matrix(tk):
    r = lax.broadcasted_iota(jnp.int32, (2 * tk, tk), 0)
    c = lax.broadcasted_iota(jnp.int32, (2 * tk, tk), 1)
    r = jnp.where(r >= tk, r - tk, r)
    return jnp.where(r > c, 1.0, 0.0).astype(BF16)


def _split_hi_lo(x):
    hi = x.astype(BF16)
    return hi, (x - hi.astype(F32)).astype(BF16)


def _sb_chains_batched(chains, acc_ref, carry_ref, hl_ref, lb_ref, mask, m2, first, keys_minor=False):
    tq = acc_ref.shape[1]
    tk = m2.shape[1]
    k_contract = 0 if keys_minor else 1
    tots = []
    for c, (get_q, get_k, _) in enumerate(chains):
        z = lax.dot_general(get_q(), get_k(), (((1,), (k_contract,)), ((), ())),
                            preferred_element_type=F32)
        l1p = jnp.log(1.0 + jnp.exp(-jnp.abs(z)))
        log_beta = jnp.minimum(z, 0.0) - l1p
        log_1m = log_beta - z
        if mask is not None:
            log_1m = jnp.where(mask, log_1m, 0.0)
        hi, lo = _split_hi_lo(log_1m)
        hl_ref[c * tq:(c + 1) * tq, :] = jnp.concatenate([hi, lo], axis=1)
        lb_ref[c] = log_beta
        tots.append(jnp.broadcast_to(jnp.sum(log_1m, axis=1, keepdims=True), (tq, 128)))
    suffix = _bdot(hl_ref[...], m2)
    for c, (_, _, get_v) in enumerate(chains):
        arg = lb_ref[c] + suffix[c * tq:(c + 1) * tq]
        if not first:
            arg = arg + (jnp.concatenate([carry_ref[c]] * (tk // 128), axis=1) if tk > 128 else carry_ref[c])
        a = jnp.exp(arg)
        if mask is not None:
            a = jnp.where(mask, a, 0.0)
        pv = lax.dot_general(a.astype(BF16), get_v(), (((1,), (1 - k_contract,)), ((), ())),
                             preferred_element_type=F32)
        if first:
            acc_ref[c] = pv
            carry_ref[c] = tots[c]
        else:
            acc_ref[c] += pv
            carry_ref[c] += tots[c]


def _head_slices():
    return [slice(h * SB_HEAD_DIM, (h + 1) * SB_HEAD_DIM) for h in range(SB_HEADS)]


def _all_underflowed(carry_ref):
    return (jnp.max(carry_ref[...]) < LOG_UNDERFLOW).astype(jnp.int32)


def _store_heads(o_ref, acc_ref):
    for h in range(SB_HEADS):
        o_ref[0, :, h * SB_HEAD_DIM:(h + 1) * SB_HEAD_DIM] = acc_ref[h].astype(BF16)


def _diag_mask(t):
    row = lax.broadcasted_iota(jnp.int32, (t, t), 0)
    col = lax.broadcasted_iota(jnp.int32, (t, t), 1)
    return col < row


def _attn_prompt_kernel(q_ref, k_ref, v_ref, o_ref, acc_ref, carry_ref, hl_ref, lb_ref, *, tq, ts, top):
    i = pl.program_id(1)
    nsub = tq // ts
    base = i * nsub
    m2 = _suffix_matrix(ts)
    heads = _head_slices()

    def chains(blocks, nrows):
        out = []
        for s in range(nsub):
            k0 = pl.multiple_of(blocks[s] * ts, ts)
            rows = slice(s * ts, s * ts + nrows)
            for sl in heads:
                out.append((lambda rows=rows, sl=sl: q_ref[0, rows, sl],
                            lambda k0=k0, sl=sl: k_ref[0, pl.ds(k0, ts), sl],
                            lambda k0=k0, sl=sl: v_ref[0, pl.ds(k0, ts), sl]))
        return out

    _sb_chains_batched(chains([base + s for s in range(nsub)], ts), acc_ref, carry_ref, hl_ref, lb_ref,
                       _diag_mask(ts), m2, True)

    def walk(k, nrows):
        nchain = nsub * SB_HEADS
        blocks = []
        for s in range(nsub):
            blk = base + s - k
            if s < nsub - 1:
                @pl.when(blk < 0)
                def _(s=s):
                    carry_ref[s * SB_HEADS:(s + 1) * SB_HEADS] = jnp.full((SB_HEADS, ts, 128), NEG_BIG, F32)
            blocks.append(jnp.maximum(blk, 0))
        _sb_chains_batched(chains(blocks, nrows), acc_ref.at[:, 0:nrows, :], carry_ref.at[:, 0:nrows, :],
                           hl_ref.at[0:nchain * nrows, :], lb_ref.at[:, 0:nrows, :], None, m2, False)

    def underflowed(lo, hi):
        return (jnp.max(carry_ref[:, lo:hi, :]) < LOG_UNDERFLOW).astype(jnp.int32)

    def more_keys(k):
        return base + nsub - 1 - k >= 0

    def full_body(st):
        k, _, _ = st
        walk(k, ts)
        return k + 1, underflowed(top, ts), underflowed(0, top)

    k, _, top_done = lax.while_loop(lambda st: more_keys(st[0]) & (st[1] == 0), full_body,
                                    (jnp.int32(1), jnp.int32(0), jnp.int32(0)))

    def top_body(st):
        k, _ = st
        walk(k, top)
        return k + 1, underflowed(0, top)

    lax.while_loop(lambda st: more_keys(st[0]) & (st[1] == 0), top_body, (k, top_done))
    for s in range(nsub):
        for h, sl in enumerate(heads):
            o_ref[0, s * ts:(s + 1) * ts, sl] = acc_ref[s * SB_HEADS + h].astype(BF16)


def _attn_prompt_call(q, kb, vb, *, tq, ts):
    b, n, w = q.shape
    nchain = SB_HEADS * (tq // ts)
    qspec = pl.BlockSpec((1, tq, w), lambda bi, i: (bi, i, 0))
    kvspec = pl.BlockSpec((1, n, w), lambda bi, i: (bi, 0, 0))
    return pl.pallas_call(
        functools.partial(_attn_prompt_kernel, tq=tq, ts=ts, top=TOP_ROWS),
        grid=(b, n // tq),
        in_specs=[qspec, kvspec, kvspec],
        out_specs=qspec,
        out_shape=jax.ShapeDtypeStruct((b, n, w), BF16),
        scratch_shapes=[pltpu.VMEM((nchain, ts, SB_HEAD_DIM), F32),
                        pltpu.VMEM((nchain, ts, 128), F32),
                        pltpu.VMEM((nchain * ts, 2 * ts), BF16),
                        pltpu.VMEM((nchain, ts, ts), F32)],
        compiler_params=pltpu.CompilerParams(
            dimension_semantics=("arbitrary", "arbitrary"), vmem_limit_bytes=VMEM_LIMIT),
        name="attn_prompt",
    )(q, kb, vb)


def _attn_sample_kernel(q_ref, k_ref, v_ref, ck_hbm, cv_hbm, o_ref, acc_ref, carry_ref, hl_ref, lb_ref,
                        kbuf, vbuf, sem, *, layer, tq, tk, nblk):
    b = pl.program_id(0)
    first_slot = b & 1

    def copies(stream, jb, slot):
        cols = pl.ds(jb * tk, tk)
        return (pltpu.make_async_copy(ck_hbm.at[layer, stream, :, :, cols], kbuf.at[slot], sem.at[0, slot]),
                pltpu.make_async_copy(cv_hbm.at[layer, stream, :, :, cols], vbuf.at[slot], sem.at[1, slot]))

    def later_slot(jb):
        return 2 + ((nblk - 1 - jb) & 1)

    @pl.when(b == 0)
    def _():
        for c in copies(b, nblk - 1, first_slot):
            c.start()

    @pl.when(b + 1 < pl.num_programs(0))
    def _():
        for c in copies(b + 1, nblk - 1, 1 - first_slot):
            c.start()

    tn = 128 * pl.cdiv(tq, 128)
    row = lax.broadcasted_iota(jnp.int32, (tq, tn), 0)
    col = lax.broadcasted_iota(jnp.int32, (tq, tn), 1)
    pad = jnp.zeros((tn - tq, SB_HEAD_DIM), BF16)
    padded = (lambda x: jnp.concatenate([x, pad], axis=0)) if tn > tq else (lambda x: x)
    new_chains = [(lambda sl=sl: q_ref[0, :, sl],
                   lambda sl=sl: padded(k_ref[0, :, sl]),
                   lambda sl=sl: padded(v_ref[0, :, sl])) for sl in _head_slices()]
    _sb_chains_batched(new_chains, acc_ref, carry_ref, hl_ref.at[:, 0:2 * tn], lb_ref.at[:, :, 0:tn],
                       col < row, _suffix_matrix(tn), True)
    m2 = _suffix_matrix(tk)

    def body(st):
        jb, _ = st
        slot = jnp.where(jb == nblk - 1, first_slot, later_slot(jb))
        for c in copies(b, jb, slot):
            c.wait()

        @pl.when(jb > 0)
        def _():
            for c in copies(b, jb - 1, later_slot(jb - 1)):
                c.start()

        chains = [(lambda sl=sl: q_ref[0, :, sl],
                   lambda h=h: kbuf[slot, h].astype(BF16),
                   lambda h=h: vbuf[slot, h].astype(BF16)) for h, sl in enumerate(_head_slices())]
        _sb_chains_batched(chains, acc_ref, carry_ref, hl_ref, lb_ref, None, m2, False, keys_minor=True)
        return jb - 1, _all_underflowed(carry_ref)

    jb_end, _ = lax.while_loop(lambda st: (st[0] >= 0) & (st[1] == 0), body,
                               (jnp.int32(nblk - 1), jnp.int32(0)))

    @pl.when(jb_end >= 0)
    def _():
        for c in copies(b, jb_end, later_slot(jb_end)):
            c.wait()

    _store_heads(o_ref, acc_ref)


def _attn_sample_call(q, kb, vb, cache_kt, cache_vt, layer, *, tk):
    b, n, w = q.shape
    past = cache_kt.shape[4]
    tk = min(tk, past)
    assert past % tk == 0 and past >= tk
    newspec = pl.BlockSpec((1, n, w), lambda bi: (bi, 0, 0))
    anyspec = pl.BlockSpec(memory_space=pl.ANY)
    return pl.pallas_call(
        functools.partial(_attn_sample_kernel, layer=layer, tq=n, tk=tk, nblk=past // tk),
        grid=(b,),
        in_specs=[newspec, newspec, newspec, anyspec, anyspec],
        out_specs=newspec,
        out_shape=jax.ShapeDtypeStruct((b, n, w), BF16),
        scratch_shapes=[pltpu.VMEM((SB_HEADS, n, SB_HEAD_DIM), F32),
                        pltpu.VMEM((SB_HEADS, n, 128), F32),
                        pltpu.VMEM((SB_HEADS * n, 2 * tk), BF16),
                        pltpu.VMEM((SB_HEADS, n, tk), F32),
                        pltpu.VMEM((4, SB_HEADS, SB_HEAD_DIM, tk), F32),
                        pltpu.VMEM((4, SB_HEADS, SB_HEAD_DIM, tk), F32),
                        pltpu.SemaphoreType.DMA((2, 4))],
        compiler_params=pltpu.CompilerParams(
            dimension_semantics=("arbitrary",), vmem_limit_bytes=VMEM_LIMIT),
        name="attn_sample",
    )(q, kb, vb, cache_kt, cache_vt)


def _outproj_kernel(x_ref, po_ref, at_ref, mod_ref, w_out_ref, g_ref, b_ref, x1_ref, mix_ref, *, bt, tn):
    if bt == 1:
        rc = _pick(tn, OUTPROJ_CHUNK)
        g1 = mod_ref[0, 2:3, :]
        for r0 in range(0, tn, rc):
            rows = slice(r0, r0 + rc)
            mix = (_bdot(po_ref[0, rows, :], w_out_ref[0:POOL_WIDTH, :])
                   + _bdot(at_ref[0, rows, :], w_out_ref[POOL_WIDTH:, :]))
            y = ALPHA * x_ref[0, rows, :] + (1.0 + g1) * mix
            x1_ref[0, rows, :] = _norm(y) * g_ref[...] + b_ref[...]
        return
    mix_ref[...] = (_bdot(po_ref[...].reshape(bt * tn, POOL_WIDTH), w_out_ref[0:POOL_WIDTH, :])
                    + _bdot(at_ref[...].reshape(bt * tn, SB_WIDTH), w_out_ref[POOL_WIDTH:, :]))
    for s in range(bt):
        rows = slice(s * tn, (s + 1) * tn)
        g1 = mod_ref[s, 2:3, :]
        y = ALPHA * x_ref[s] + (1.0 + g1) * mix_ref[rows, :]
        x1_ref[s] = _norm(y) * g_ref[...] + b_ref[...]


def _outproj_call(x, po, at, mod, w_out, ln_g, ln_b, *, bt, tn):
    b, n, d = x.shape
    tok = lambda width: pl.BlockSpec((bt, tn, width), lambda bi, j: (bi, j, 0))
    const2 = lambda s: pl.BlockSpec(s, lambda bi, j: (0, 0))
    return pl.pallas_call(
        functools.partial(_outproj_kernel, bt=bt, tn=tn),
        grid=(b // bt, n // tn),
        in_specs=[tok(d), tok(POOL_WIDTH), tok(SB_WIDTH),
                  pl.BlockSpec((bt, 6, d), lambda bi, j: (bi, 0, 0)),
                  const2(w_out.shape), const2((1, d)), const2((1, d))],
        out_specs=tok(d),
        out_shape=jax.ShapeDtypeStruct((b, n, d), F32),
        scratch_shapes=[pltpu.VMEM((bt * tn, d), F32)],
        compiler_params=pltpu.CompilerParams(
            dimension_semantics=("arbitrary", "arbitrary"), vmem_limit_bytes=VMEM_LIMIT),
        name="outproj",
    )(x, po, at, mod, w_out, ln_g, ln_b)


def _route(logits):
    col = lax.broadcasted_iota(jnp.int32, logits.shape, 1)
    colf = col.astype(F32)
    is_group = col < N_GROUPS
    gl = jnp.where(is_group, logits, NEG_BIG)
    gmax = jnp.max(gl, axis=1, keepdims=True)
    gsum = jnp.sum(jnp.where(is_group, jnp.exp(gl - gmax), 0.0), axis=1, keepdims=True)
    pg = 1.0 / gsum
    gidx = jnp.min(jnp.where(gl == gmax, colf, 1e9), axis=1, keepdims=True)
    egrp = ((col - N_GROUPS) >> 2).astype(F32)
    in_group = (col >= N_GROUPS) & (col < N_GROUPS + N_EXPERTS) & (egrp == gidx)
    el = jnp.where(in_group, logits, NEG_BIG)
    m1 = jnp.max(el, axis=1, keepdims=True)
    i1 = jnp.min(jnp.where(el == m1, colf, 1e9), axis=1, keepdims=True)
    el2 = jnp.where(colf == i1, NEG_BIG, el)
    m2 = jnp.max(el2, axis=1, keepdims=True)
    i2 = jnp.min(jnp.where(el2 == m2, colf, 1e9), axis=1, keepdims=True)
    r = jnp.exp(m2 - m1)
    w1 = pg / (1.0 + r)
    w2 = pg * r / (1.0 + r)
    return jnp.where(colf == i1, w1, 0.0) + jnp.where(colf == i2, w2, 0.0)


def _moe_prologue(x, sh2, sc2, wr_ref, br_ref):
    hn = _norm(x) * (1.0 + sc2) + sh2
    hi, lo = _split_hi_lo(hn)
    w = wr_ref[0]
    p = _bdot(hi, w)
    logits = p[:, 0:ROUTER_LANES] + p[:, ROUTER_LANES:] + _bdot(lo, w[:, 0:ROUTER_LANES]) + br_ref[0]
    return hi, _route(logits)


def _moe_kernel(x1_ref, mod_ref, wr_ref, br_ref, wup_ref, wdn_ref, g_ref, b_ref, o_ref,
                hn_ref, gate_ref, acc_ref, *, bt, tn):
    g = pl.program_id(2)

    @pl.when(g == 0)
    def _():
        for s in range(bt):
            hn, gate = _moe_prologue(x1_ref[s], mod_ref[s, 3:4, :], mod_ref[s, 4:5, :], wr_ref, br_ref)
            hn_ref[s * tn:(s + 1) * tn, :] = hn
            gate_ref[s * tn:(s + 1) * tn, :] = gate

    hn = hn_ref[...]
    gate = gate_ref[...]
    col = lax.broadcasted_iota(jnp.int32, gate.shape, 1)
    y = None
    for e in range(EXPERTS_PER_GROUP):
        hid = _bdot(hn, wup_ref[0, e])
        a = hid[:, 0:D_EXPERT]
        u = hid[:, D_EXPERT:]
        lane = N_GROUPS + g * EXPERTS_PER_GROUP + e
        gcol = jnp.sum(jnp.where(col == lane, gate, 0.0), axis=1, keepdims=True)
        act = a * (1.0 / (1.0 + jnp.exp(-a))) * u * gcol
        part = _bdot(act.astype(BF16), wdn_ref[0, e])
        y = part if y is None else y + part

    @pl.when(g == 0)
    def _():
        acc_ref[...] = y

    @pl.when(g > 0)
    def _():
        acc_ref[...] += y

    @pl.when(g == pl.num_programs(2) - 1)
    def _():
        for s in range(bt):
            g2 = mod_ref[s, 5:6, :]
            y2 = ALPHA * x1_ref[s] + (1.0 + g2) * acc_ref[s * tn:(s + 1) * tn, :]
            o_ref[s] = _norm(y2) * g_ref[0] + b_ref[0]


def _moe_call(x1, mod, w_r, b_r, w_up, w_down, ln_g, ln_b, layer, *, bt, tn):
    b, n, d = x1.shape
    tm = bt * tn
    tok = pl.BlockSpec((bt, tn, d), lambda bi, j, g: (bi, j, 0))
    per_layer = lambda a: pl.BlockSpec((1,) + a.shape[1:], lambda bi, j, g: (layer,) + (0,) * (a.ndim - 1))
    grp = lambda w: pl.BlockSpec((1, EXPERTS_PER_GROUP) + w.shape[2:], lambda bi, j, g: (layer, g, 0, 0))
    return pl.pallas_call(
        functools.partial(_moe_kernel, bt=bt, tn=tn),
        grid=(b // bt, n // tn, N_GROUPS),
        in_specs=[tok,
                  pl.BlockSpec((bt, 6, d), lambda bi, j, g: (bi, 0, 0)),
                  per_layer(w_r), per_layer(b_r),
                  grp(w_up), grp(w_down),
                  per_layer(ln_g), per_layer(ln_b)],
        out_specs=tok,
        out_shape=jax.ShapeDtypeStruct((b, n, d), F32),
        scratch_shapes=[pltpu.VMEM((tm, d), BF16),
                        pltpu.VMEM((tm, ROUTER_LANES), F32),
                        pltpu.VMEM((tm, d), F32)],
        compiler_params=pltpu.CompilerParams(
            dimension_semantics=("arbitrary", "arbitrary", "arbitrary"), vmem_limit_bytes=VMEM_LIMIT),
        name="moe",
    )(x1, mod, w_r, b_r, w_up, w_down, ln_g, ln_b)


def _pick(n, pref):
    t = min(n, pref)
    while n % t:
        t //= 2
    return t


def kernel(x_prompt, x_sample, c_prompt, c_sample, cache_k, cache_v, state_pool, w_mod, b_mod, w_in, w_pool, pool_scale, w_out, ln1_g, ln1_b, ln2_g, ln2_b, w_group, b_group, w_router, b_router, w_up, w_down):
    depth = w_mod.shape[0]
    bp, seq, d = x_prompt.shape
    bs, dseq, _ = x_sample.shape
    past = cache_k.shape[2]

    w_mod_b = w_mod.astype(BF16)
    w_in_b = w_in.astype(BF16)
    w_pool_b = w_pool.astype(BF16)
    w_out_b = w_out.astype(BF16)
    w_up_b = w_up.astype(BF16)
    w_down_b = w_down.astype(BF16)
    pad = jnp.zeros((depth, d, ROUTER_LANES - N_GROUPS - N_EXPERTS), F32)
    w_r32 = jnp.concatenate([w_group, w_router, pad], axis=-1)
    w_r_hi = w_r32.astype(BF16)
    w_r = jnp.concatenate([w_r_hi, (w_r32 - w_r_hi.astype(F32)).astype(BF16)], axis=-1)
    b_r = jnp.concatenate([b_group, b_router, pad[:, 0, :]], axis=-1).reshape(depth, 1, ROUTER_LANES)

    cache_kt = jnp.transpose(cache_k, (0, 1, 3, 4, 2))
    cache_vt = jnp.transpose(cache_v, (0, 1, 3, 4, 2))

    ln2_g3 = ln2_g.reshape(depth, 1, d)
    ln2_b3 = ln2_b.reshape(depth, 1, d)

    mod_all = _mod_call(jnp.concatenate([c_prompt, c_sample], axis=0), w_mod_b, b_mod)
    mod_p = mod_all[:, :bp].reshape(depth, bp, 6, d)
    mod_s = mod_all[:, bp:].reshape(depth, bs, 6, d)

    tm_p = _pick(seq, 1024)
    tq_p = _pick(seq, 512)
    tn_p = _pick(seq, 1024)
    bt_s = _pick(bs, max(1, 1024 // dseq))

    def run_layer(l, x, mod, hist16, pos0, is_prompt, kv_prev=None):
        inproj_args = (x, mod, hist16, w_in_b[l], w_pool_b[l], pool_scale[l].reshape(1, POOL_WIDTH))
        if is_prompt:
            q, k, v, kb, vb, po, new_pool = _inproj_call(*inproj_args, kv_prev, tm=tm_p, pos0=pos0)
        else:
            q, k, v, kb, vb, po, new_pool = _inproj_short_call(
                *inproj_args, bt=_pick(bs, max(1, INPROJ_CHUNK // dseq)), pos0=pos0)
        if is_prompt:
            at = _attn_prompt_call(q, kb, vb, tq=tq_p, ts=min(tq_p, 128))
            bt, tn = 1, tn_p
        else:
            at = _attn_sample_call(q, kb, vb, cache_kt, cache_vt, l, tk=256)
            bt, tn = bt_s, dseq
        x1 = _outproj_call(x, po, at, mod, w_out_b[l], ln1_g[l].reshape(1, d), ln1_b[l].reshape(1, d),
                           bt=bt, tn=tn)
        x2 = _moe_call(x1, mod, w_r, b_r, w_up_b, w_down_b, ln2_g3, ln2_b3, l, bt=bt, tn=tn)
        heads = lambda a: a.reshape(a.shape[:-1] + (SB_HEADS, SB_HEAD_DIM))
        return x2, heads(k), heads(v), new_pool[:, HIST_ROWS - POOL_HIST:]

    y = x_prompt
    pp = []
    kv_prev = None
    zero_hist = jnp.zeros((bp, HIST_ROWS, POOL_WIDTH), F32)
    for l in range(depth):
        y, k_all, v_all, p_new = run_layer(l, y, mod_p[l], zero_hist, 0, True, kv_prev)
        kv_prev = (k_all.reshape(-1, bp, seq, SB_WIDTH), v_all.reshape(-1, bp, seq, SB_WIDTH))
        pp.append(p_new)
    y_prompt = y
    k_prompt = k_all.reshape(depth, bp, seq, SB_HEADS, SB_HEAD_DIM)
    v_prompt = v_all.reshape(depth, bp, seq, SB_HEADS, SB_HEAD_DIM)

    y = x_sample
    kd, vd, pd = [], [], []
    hist_s = jnp.pad(state_pool, ((0, 0), (0, 0), (HIST_ROWS - POOL_HIST, 0), (0, 0)))
    for l in range(depth):
        y, k_new, v_new, p_new = run_layer(l, y, mod_s[l], hist_s[l], past, False)
        kd.append(k_new); vd.append(v_new); pd.append(p_new)
    y_sample = y

    return (y_prompt, y_sample, k_prompt, v_prompt, jnp.stack(pp),
            jnp.stack(kd), jnp.stack(vd), jnp.stack(pd))
```

```python
import functools

import jax
import jax.numpy as jnp
from jax import lax
from jax.experimental import pallas as pl
from jax.experimental.pallas import tpu as pltpu

F32 = jnp.float32
BF16 = jnp.bfloat16

D_MODEL = 1024
POOL_WINDOWS = (2, 4, 8, 16)
POOL_WIDTH = 512
POOL_GROUP_DIM = 128
POOL_HIST = 15
HIST_ROWS = 16
SB_WIDTH = 512
SB_HEADS = 8
SB_HEAD_DIM = 64
N_GROUPS = 4
EXPERTS_PER_GROUP = 4
N_EXPERTS = 16
D_EXPERT = 256
ROUTER_LANES = 128
DEPTH = 2
ALPHA = (2 * DEPTH) ** 0.25
LN_EPS = 1e-5
QK_SCALE = 1.0 / 8.0
LOG_UNDERFLOW = -104.0
NEG_BIG = -1e30
VMEM_LIMIT = 56 * 1024 * 1024
INPROJ_CHUNK = 512
OUTPROJ_CHUNK = 256
MOE_TILE = 512
TOP_ROWS = 64


def _norm(x):
    mu = jnp.mean(x, axis=-1, keepdims=True)
    xc = x - mu
    var = jnp.mean(xc * xc, axis=-1, keepdims=True)
    return xc * lax.rsqrt(var + LN_EPS)


def _bdot(a, b):
    return jnp.dot(a, b, preferred_element_type=F32)


def _mod_kernel(c_ref, w_ref, b_ref, o_ref):
    o_ref[0] = _bdot(c_ref[...].astype(BF16), w_ref[0]) + b_ref[0]


def _mod_call(c_all, w_mod, b_mod):
    depth, d, n6 = w_mod.shape
    bc = c_all.shape[0]
    tn = 1536
    return pl.pallas_call(
        _mod_kernel,
        grid=(depth, n6 // tn),
        in_specs=[
            pl.BlockSpec((bc, d), lambda l, j: (0, 0)),
            pl.BlockSpec((1, d, tn), lambda l, j: (l, 0, j)),
            pl.BlockSpec((1, 1, tn), lambda l, j: (l, 0, j)),
        ],
        out_specs=pl.BlockSpec((1, bc, tn), lambda l, j: (l, 0, j)),
        out_shape=jax.ShapeDtypeStruct((depth, bc, n6), F32),
        name="mod",
    )(c_all, w_mod, b_mod.reshape(depth, 1, n6))


def _pool_rows(ue_ref, base, u, pos0, w_pool_ref, ps_ref):
    nrows = u.shape[0]
    pos = pos0 + lax.broadcasted_iota(jnp.int32, (nrows, 1), 0)
    outs = []
    for g, w in enumerate(POOL_WINDOWS):
        c0 = g * POOL_GROUP_DIM
        c1 = c0 + POOL_GROUP_DIM
        acc = u[:, c0:c1]
        for dlt in range(1, w):
            acc = acc + ue_ref[base - dlt:base - dlt + nrows, c0:c1]
        cnt = jnp.minimum(w, pos + 1).astype(F32)
        pooled = acc / cnt - u[:, c0:c1]
        mixed = _bdot(pooled.astype(BF16), w_pool_ref[g])
        outs.append(mixed * ps_ref[:, c0:c1])
    return jnp.concatenate(outs, axis=1).astype(BF16)


def _inproj_short_kernel(x_ref, mod_ref, hist_ref, w_in_ref, w_pool_ref, ps_ref,
                         q_ref, k_ref, v_ref, kb_ref, vb_ref, po_ref, np_ref, ue_ref, *, bt, tn, pos0):
    hn = jnp.concatenate(
        [(_norm(x_ref[s]) * (1.0 + mod_ref[s, 1:2, :]) + mod_ref[s, 0:1, :]).astype(BF16) for s in range(bt)],
        axis=0)
    proj = _bdot(hn, w_in_ref[...])
    stride = HIST_ROWS + tn
    for s in range(bt):
        p = proj[s * tn:(s + 1) * tn]
        u = p[:, 0:POOL_WIDTH]
        k = p[:, POOL_WIDTH + SB_WIDTH:POOL_WIDTH + 2 * SB_WIDTH]
        v = p[:, POOL_WIDTH + 2 * SB_WIDTH:]
        q_ref[s] = (p[:, POOL_WIDTH:POOL_WIDTH + SB_WIDTH] * QK_SCALE).astype(BF16)
        k_ref[s] = k
        v_ref[s] = v
        kb_ref[s] = k.astype(BF16)
        vb_ref[s] = v.astype(BF16)
        base = s * stride + HIST_ROWS
        ue_ref[s * stride:base, :] = hist_ref[s]
        ue_ref[base:base + tn, :] = u
        po_ref[s] = _pool_rows(ue_ref, base, u, pos0, w_pool_ref, ps_ref)
        np_ref[s] = ue_ref[base + tn - HIST_ROWS:base + tn, :]


def _inproj_short_call(x, mod, hist16, w_in, w_pool, pool_scale, *, bt, pos0):
    b, n, d = x.shape
    nproj = w_in.shape[1]
    blk = lambda rows, width: pl.BlockSpec((bt, rows, width), lambda bi: (bi, 0, 0))
    const2 = lambda s: pl.BlockSpec(s, lambda bi: (0, 0))
    return pl.pallas_call(
        functools.partial(_inproj_short_kernel, bt=bt, tn=n, pos0=pos0),
        grid=(b // bt,),
        in_specs=[blk(n, d), blk(6, d), blk(HIST_ROWS, POOL_WIDTH), const2((d, nproj)),
                  pl.BlockSpec(w_pool.shape, lambda bi: (0, 0, 0)), const2((1, POOL_WIDTH))],
        out_specs=[blk(n, SB_WIDTH)] * 5 + [blk(n, POOL_WIDTH), blk(HIST_ROWS, POOL_WIDTH)],
        out_shape=[
            jax.ShapeDtypeStruct((b, n, SB_WIDTH), BF16),
            jax.ShapeDtypeStruct((b, n, SB_WIDTH), F32),
            jax.ShapeDtypeStruct((b, n, SB_WIDTH), F32),
            jax.ShapeDtypeStruct((b, n, SB_WIDTH), BF16),
            jax.ShapeDtypeStruct((b, n, SB_WIDTH), BF16),
            jax.ShapeDtypeStruct((b, n, POOL_WIDTH), BF16),
            jax.ShapeDtypeStruct((b, HIST_ROWS, POOL_WIDTH), F32),
        ],
        scratch_shapes=[pltpu.VMEM((bt * (HIST_ROWS + n), POOL_WIDTH), F32)],
        compiler_params=pltpu.CompilerParams(
            dimension_semantics=("arbitrary",), vmem_limit_bytes=VMEM_LIMIT),
        name="inproj_short",
    )(x, mod, hist16, w_in, w_pool, pool_scale)


def _inproj_kernel(x_ref, mod_ref, hist_ref, w_in_ref, w_pool_ref, ps_ref, *refs, tm, rc, pos0, nprev):
    if nprev:
        kprev_ref, vprev_ref, q_ref, kst_ref, vst_ref, kb_ref, vb_ref, po_ref, np_ref, ue_ref = refs
        kst_ref[0:nprev] = kprev_ref[...]
        vst_ref[0:nprev] = vprev_ref[...]
        k_ref, v_ref = kst_ref.at[nprev], vst_ref.at[nprev]
    else:
        q_ref, k_ref, v_ref, kb_ref, vb_ref, po_ref, np_ref, ue_ref = refs
    j = pl.program_id(1)
    sh1 = mod_ref[0, 0:1, :]
    sc1 = mod_ref[0, 1:2, :]

    @pl.when(j == 0)
    def _():
        ue_ref[0:HIST_ROWS, :] = hist_ref[0]

    for r0 in range(0, tm, rc):
        rows = slice(r0, r0 + rc)
        hn = (_norm(x_ref[0, rows, :]) * (1.0 + sc1) + sh1).astype(BF16)
        proj = _bdot(hn, w_in_ref[...])
        u = proj[:, 0:POOL_WIDTH]
        q = proj[:, POOL_WIDTH:POOL_WIDTH + SB_WIDTH]
        k = proj[:, POOL_WIDTH + SB_WIDTH:POOL_WIDTH + 2 * SB_WIDTH]
        v = proj[:, POOL_WIDTH + 2 * SB_WIDTH:]
        q_ref[0, rows, :] = (q * QK_SCALE).astype(BF16)
        k_ref[0, rows, :] = k
        v_ref[0, rows, :] = v
        kb_ref[0, rows, :] = k.astype(BF16)
        vb_ref[0, rows, :] = v.astype(BF16)
        ue_ref[HIST_ROWS + r0:HIST_ROWS + r0 + rc, :] = u
        po_ref[0, rows, :] = _pool_rows(ue_ref, HIST_ROWS + r0, u, pos0 + j * tm + r0, w_pool_ref, ps_ref)

    tail = ue_ref[tm:tm + HIST_ROWS, :]
    ue_ref[0:HIST_ROWS, :] = tail

    @pl.when(j == pl.num_programs(1) - 1)
    def _():
        np_ref[0] = tail


def _inproj_call(x, mod, hist16, w_in, w_pool, pool_scale, kv_prev=None, *, tm, pos0):
    b, n, d = x.shape
    nproj = w_in.shape[1]
    tok = lambda width: pl.BlockSpec((1, tm, width), lambda bi, j: (bi, j, 0))
    per_b = lambda rows, width: pl.BlockSpec((1, rows, width), lambda bi, j: (bi, 0, 0))
    const2 = lambda s: pl.BlockSpec(s, lambda bi, j: (0, 0))
    prev = [a.reshape((-1, b, n, SB_WIDTH)) for a in (kv_prev or ())]
    nprev = prev[0].shape[0] if prev else 0
    stack = lambda layers: pl.BlockSpec((layers, 1, tm, SB_WIDTH), lambda bi, j: (0, bi, j, 0))
    kv_spec = stack(nprev + 1) if nprev else tok(SB_WIDTH)
    kv_shape = (nprev + 1, b, n, SB_WIDTH) if nprev else (b, n, SB_WIDTH)
    return pl.pallas_call(
        functools.partial(_inproj_kernel, tm=tm, rc=_pick(tm, INPROJ_CHUNK), pos0=pos0, nprev=nprev),
        grid=(b, n // tm),
        in_specs=[
            tok(d),
            per_b(6, d),
            per_b(HIST_ROWS, POOL_WIDTH),
            const2((d, nproj)),
            pl.BlockSpec(w_pool.shape, lambda bi, j: (0, 0, 0)),
            const2((1, POOL_WIDTH)),
        ] + [stack(nprev)] * len(prev),
        out_specs=[tok(SB_WIDTH), kv_spec, kv_spec, tok(SB_WIDTH), tok(SB_WIDTH),
                   tok(POOL_WIDTH), per_b(HIST_ROWS, POOL_WIDTH)],
        out_shape=[
            jax.ShapeDtypeStruct((b, n, SB_WIDTH), BF16),
            jax.ShapeDtypeStruct(kv_shape, F32),
            jax.ShapeDtypeStruct(kv_shape, F32),
            jax.ShapeDtypeStruct((b, n, SB_WIDTH), BF16),
            jax.ShapeDtypeStruct((b, n, SB_WIDTH), BF16),
            jax.ShapeDtypeStruct((b, n, POOL_WIDTH), BF16),
            jax.ShapeDtypeStruct((b, HIST_ROWS, POOL_WIDTH), F32),
        ],
        scratch_shapes=[pltpu.VMEM((tm + HIST_ROWS, POOL_WIDTH), F32)],
        compiler_params=pltpu.CompilerParams(
            dimension_semantics=("arbitrary", "arbitrary"), vmem_limit_bytes=VMEM_LIMIT),
        name="inproj",
    )(x, mod, hist16, w_in, w_pool, pool_scale, *prev)


def _suffix_matrix(tk):
    r = lax.broadcasted_iota(jnp.int32, (2 * tk, tk), 0)
    c = lax.broadcasted_iota(jnp.int32, (2 * tk, tk), 1)
    r = jnp.where(r >= tk, r - tk, r)
    return jnp.where(r > c, 1.0, 0.0).astype(BF16)


def _split_hi_lo(x):
    hi = x.astype(BF16)
    return hi, (x - hi.astype(F32)).astype(BF16)


def _sb_chains_batched(chains, acc_ref, carry_ref, hl_ref, lb_ref, mask, m2, first, keys_minor=False):
    tq = acc_ref.shape[1]
    tk = m2.shape[1]
    k_contract = 0 if keys_minor else 1
    tots = []
    for c, (get_q, get_k, _) in enumerate(chains):
        z = lax.dot_general(get_q(), get_k(), (((1,), (k_contract,)), ((), ())),
                            preferred_element_type=F32)
        l1p = jnp.log(1.0 + jnp.exp(-jnp.abs(z)))
        log_beta = jnp.minimum(z, 0.0) - l1p
        log_1m = log_beta - z
        if mask is not None:
            log_1m = jnp.where(mask, log_1m, 0.0)
        hi, lo = _split_hi_lo(log_1m)
        hl_ref[c * tq:(c + 1) * tq, :] = jnp.concatenate([hi, lo], axis=1)
        lb_ref[c] = log_beta
        tots.append(jnp.broadcast_to(jnp.sum(log_1m, axis=1, keepdims=True), (tq, 128)))
    suffix = _bdot(hl_ref[...], m2)
    for c, (_, _, get_v) in enumerate(chains):
        arg = lb_ref[c] + suffix[c * tq:(c + 1) * tq]
        if not first:
            arg = arg + (jnp.concatenate([carry_ref[c]] * (tk // 128), axis=1) if tk > 128 else carry_ref[c])
        a = jnp.exp(arg)
        if mask is not None:
            a = jnp.where(mask, a, 0.0)
        pv = lax.dot_general(a.astype(BF16), get_v(), (((1,), (1 - k_contract,)), ((), ())),
                             preferred_element_type=F32)
        if first:
            acc_ref[c] = pv
            carry_ref[c] = tots[c]
        else:
            acc_ref[c] += pv
            carry_ref[c] += tots[c]


def _head_slices():
    return [slice(h * SB_HEAD_DIM, (h + 1) * SB_HEAD_DIM) for h in range(SB_HEADS)]


def _all_underflowed(carry_ref):
    return (jnp.max(carry_ref[...]) < LOG_UNDERFLOW).astype(jnp.int32)


def _store_heads(o_ref, acc_ref):
    for h in range(SB_HEADS):
        o_ref[0, :, h * SB_HEAD_DIM:(h + 1) * SB_HEAD_DIM] = acc_ref[h].astype(BF16)


def _diag_mask(t):
    row = lax.broadcasted_iota(jnp.int32, (t, t), 0)
    col = lax.broadcasted_iota(jnp.int32, (t, t), 1)
    return col < row


def _attn_prompt_kernel(q_ref, k_ref, v_ref, o_ref, acc_ref, carry_ref, hl_ref, lb_ref, *, tq, ts, top):
    i = pl.program_id(1)
    nsub = tq // ts
    base = i * nsub
    m2 = _suffix_matrix(ts)
    heads = _head_slices()

    def chains(blocks, nrows):
        out = []
        for s in range(nsub):
            k0 = pl.multiple_of(blocks[s] * ts, ts)
            rows = slice(s * ts, s * ts + nrows)
            for sl in heads:
                out.append((lambda rows=rows, sl=sl: q_ref[0, rows, sl],
                            lambda k0=k0, sl=sl: k_ref[0, pl.ds(k0, ts), sl],
                            lambda k0=k0, sl=sl: v_ref[0, pl.ds(k0, ts), sl]))
        return out

    _sb_chains_batched(chains([base + s for s in range(nsub)], ts), acc_ref, carry_ref, hl_ref, lb_ref,
                       _diag_mask(ts), m2, True)

    def walk(k, nrows):
        nchain = nsub * SB_HEADS
        blocks = []
        for s in range(nsub):
            blk = base + s - k
            if s < nsub - 1:
                @pl.when(blk < 0)
                def _(s=s):
                    carry_ref[s * SB_HEADS:(s + 1) * SB_HEADS] = jnp.full((SB_HEADS, ts, 128), NEG_BIG, F32)
            blocks.append(jnp.maximum(blk, 0))
        _sb_chains_batched(chains(blocks, nrows), acc_ref.at[:, 0:nrows, :], carry_ref.at[:, 0:nrows, :],
                           hl_ref.at[0:nchain * nrows, :], lb_ref.at[:, 0:nrows, :], None, m2, False)

    def underflowed(lo, hi):
        return (jnp.max(carry_ref[:, lo:hi, :]) < LOG_UNDERFLOW).astype(jnp.int32)

    def more_keys(k):
        return base + nsub - 1 - k >= 0

    def full_body(st):
        k, _, _ = st
        walk(k, ts)
        return k + 1, underflowed(top, ts), underflowed(0, top)

    k, _, top_done = lax.while_loop(lambda st: more_keys(st[0]) & (st[1] == 0), full_body,
                                    (jnp.int32(1), jnp.int32(0), jnp.int32(0)))

    def top_body(st):
        k, _ = st
        walk(k, top)
        return k + 1, underflowed(0, top)

    lax.while_loop(lambda st: more_keys(st[0]) & (st[1] == 0), top_body, (k, top_done))
    for s in range(nsub):
        for h, sl in enumerate(heads):
            o_ref[0, s * ts:(s + 1) * ts, sl] = acc_ref[s * SB_HEADS + h].astype(BF16)


def _attn_prompt_call(q, kb, vb, *, tq, ts):
    b, n, w = q.shape
    nchain = SB_HEADS * (tq // ts)
    qspec = pl.BlockSpec((1, tq, w), lambda bi, i: (bi, i, 0))
    kvspec = pl.BlockSpec((1, n, w), lambda bi, i: (bi, 0, 0))
    return pl.pallas_call(
        functools.partial(_attn_prompt_kernel, tq=tq, ts=ts, top=TOP_ROWS),
        grid=(b, n // tq),
        in_specs=[qspec, kvspec, kvspec],
        out_specs=qspec,
        out_shape=jax.ShapeDtypeStruct((b, n, w), BF16),
        scratch_shapes=[pltpu.VMEM((nchain, ts, SB_HEAD_DIM), F32),
                        pltpu.VMEM((nchain, ts, 128), F32),
                        pltpu.VMEM((nchain * ts, 2 * ts), BF16),
                        pltpu.VMEM((nchain, ts, ts), F32)],
        compiler_params=pltpu.CompilerParams(
            dimension_semantics=("arbitrary", "arbitrary"), vmem_limit_bytes=VMEM_LIMIT),
        name="attn_prompt",
    )(q, kb, vb)


def _attn_sample_kernel(q_ref, k_ref, v_ref, ck_hbm, cv_hbm, o_ref, acc_ref, carry_ref, hl_ref, lb_ref,
                        kbuf, vbuf, sem, *, layer, tq, tk, nblk):
    b = pl.program_id(0)
    first_slot = b & 1

    def copies(stream, jb, slot):
        cols = pl.ds(jb * tk, tk)
        return (pltpu.make_async_copy(ck_hbm.at[layer, stream, :, :, cols], kbuf.at[slot], sem.at[0, slot]),
                pltpu.make_async_copy(cv_hbm.at[layer, stream, :, :, cols], vbuf.at[slot], sem.at[1, slot]))

    def later_slot(jb):
        return 2 + ((nblk - 1 - jb) & 1)

    @pl.when(b == 0)
    def _():
        for c in copies(b, nblk - 1, first_slot):
            c.start()

    @pl.when(b + 1 < pl.num_programs(0))
    def _():
        for c in copies(b + 1, nblk - 1, 1 - first_slot):
            c.start()

    tn = 128 * pl.cdiv(tq, 128)
    row = lax.broadcasted_iota(jnp.int32, (tq, tn), 0)
    col = lax.broadcasted_iota(jnp.int32, (tq, tn), 1)
    pad = jnp.zeros((tn - tq, SB_HEAD_DIM), BF16)
    padded = (lambda x: jnp.concatenate([x, pad], axis=0)) if tn > tq else (lambda x: x)
    new_chains = [(lambda sl=sl: q_ref[0, :, sl],
                   lambda sl=sl: padded(k_ref[0, :, sl]),
                   lambda sl=sl: padded(v_ref[0, :, sl])) for sl in _head_slices()]
    _sb_chains_batched(new_chains, acc_ref, carry_ref, hl_ref.at[:, 0:2 * tn], lb_ref.at[:, :, 0:tn],
                       col < row, _suffix_matrix(tn), True)
    m2 = _suffix_matrix(tk)

    def body(st):
        jb, _ = st
        slot = jnp.where(jb == nblk - 1, first_slot, later_slot(jb))
        for c in copies(b, jb, slot):
            c.wait()

        @pl.when(jb > 0)
        def _():
            for c in copies(b, jb - 1, later_slot(jb - 1)):
                c.start()

        chains = [(lambda sl=sl: q_ref[0, :, sl],
                   lambda h=h: kbuf[slot, h].astype(BF16),
                   lambda h=h: vbuf[slot, h].astype(BF16)) for h, sl in enumerate(_head_slices())]
        _sb_chains_batched(chains, acc_ref, carry_ref, hl_ref, lb_ref, None, m2, False, keys_minor=True)
        return jb - 1, _all_underflowed(carry_ref)

    jb_end, _ = lax.while_loop(lambda st: (st[0] >= 0) & (st[1] == 0), body,
                               (jnp.int32(nblk - 1), jnp.int32(0)))

    @pl.when(jb_end >= 0)
    def _():
        for c in copies(b, jb_end, later_slot(jb_end)):
            c.wait()

    _store_heads(o_ref, acc_ref)


def _attn_sample_call(q, kb, vb, cache_kt, cache_vt, layer, *, tk):
    b, n, w = q.shape
    past = cache_kt.shape[4]
    tk = min(tk, past)
    assert past % tk == 0 and past >= tk
    newspec = pl.BlockSpec((1, n, w), lambda bi: (bi, 0, 0))
    anyspec = pl.BlockSpec(memory_space=pl.ANY)
    return pl.pallas_call(
        functools.partial(_attn_sample_kernel, layer=layer, tq=n, tk=tk, nblk=past // tk),
        grid=(b,),
        in_specs=[newspec, newspec, newspec, anyspec, anyspec],
        out_specs=newspec,
        out_shape=jax.ShapeDtypeStruct((b, n, w), BF16),
        scratch_shapes=[pltpu.VMEM((SB_HEADS, n, SB_HEAD_DIM), F32),
                        pltpu.VMEM((SB_HEADS, n, 128), F32),
                        pltpu.VMEM((SB_HEADS * n, 2 * tk), BF16),
                        pltpu.VMEM((SB_HEADS, n, tk), F32),
                        pltpu.VMEM((4, SB_HEADS, SB_HEAD_DIM, tk), F32),
                        pltpu.VMEM((4, SB_HEADS, SB_HEAD_DIM, tk), F32),
                        pltpu.SemaphoreType.DMA((2, 4))],
        compiler_params=pltpu.CompilerParams(
            dimension_semantics=("arbitrary",), vmem_limit_bytes=VMEM_LIMIT),
        name="attn_sample",
    )(q, kb, vb, cache_kt, cache_vt)


def _outproj_kernel(x_ref, po_ref, at_ref, mod_ref, w_out_ref, g_ref, b_ref, x1_ref, mix_ref, *, bt, tn):
    if bt == 1:
        rc = _pick(tn, OUTPROJ_CHUNK)
        g1 = mod_ref[0, 2:3, :]
        for r0 in range(0, tn, rc):
            rows = slice(r0, r0 + rc)
            mix = (_bdot(po_ref[0, rows, :], w_out_ref[0:POOL_WIDTH, :])
                   + _bdot(at_ref[0, rows, :], w_out_ref[POOL_WIDTH:, :]))
            y = ALPHA * x_ref[0, rows, :] + (1.0 + g1) * mix
            x1_ref[0, rows, :] = _norm(y) * g_ref[...] + b_ref[...]
        return
    mix_ref[...] = (_bdot(po_ref[...].reshape(bt * tn, POOL_WIDTH), w_out_ref[0:POOL_WIDTH, :])
                    + _bdot(at_ref[...].reshape(bt * tn, SB_WIDTH), w_out_ref[POOL_WIDTH:, :]))
    for s in range(bt):
        rows = slice(s * tn, (s + 1) * tn)
        g1 = mod_ref[s, 2:3, :]
        y = ALPHA * x_ref[s] + (1.0 + g1) * mix_ref[rows, :]
        x1_ref[s] = _norm(y) * g_ref[...] + b_ref[...]


def _outproj_call(x, po, at, mod, w_out, ln_g, ln_b, *, bt, tn):
    b, n, d = x.shape
    tok = lambda width: pl.BlockSpec((bt, tn, width), lambda bi, j: (bi, j, 0))
    const2 = lambda s: pl.BlockSpec(s, lambda bi, j: (0, 0))
    return pl.pallas_call(
        functools.partial(_outproj_kernel, bt=bt, tn=tn),
        grid=(b // bt, n // tn),
        in_specs=[tok(d), tok(POOL_WIDTH), tok(SB_WIDTH),
                  pl.BlockSpec((bt, 6, d), lambda bi, j: (bi, 0, 0)),
                  const2(w_out.shape), const2((1, d)), const2((1, d))],
        out_specs=tok(d),
        out_shape=jax.ShapeDtypeStruct((b, n, d), F32),
        scratch_shapes=[pltpu.VMEM((bt * tn, d), F32)],
        compiler_params=pltpu.CompilerParams(
            dimension_semantics=("arbitrary", "arbitrary"), vmem_limit_bytes=VMEM_LIMIT),
        name="outproj",
    )(x, po, at, mod, w_out, ln_g, ln_b)


def _route(logits):
    col = lax.broadcasted_iota(jnp.int32, logits.shape, 1)
    colf = col.astype(F32)
    is_group = col < N_GROUPS
    gl = jnp.where(is_group, logits, NEG_BIG)
    gmax = jnp.max(gl, axis=1, keepdims=True)
    gsum = jnp.sum(jnp.where(is_group, jnp.exp(gl - gmax), 0.0), axis=1, keepdims=True)
    pg = 1.0 / gsum
    gidx = jnp.min(jnp.where(gl == gmax, colf, 1e9), axis=1, keepdims=True)
    egrp = ((col - N_GROUPS) >> 2).astype(F32)
    in_group = (col >= N_GROUPS) & (col < N_GROUPS + N_EXPERTS) & (egrp == gidx)
    el = jnp.where(in_group, logits, NEG_BIG)
    m1 = jnp.max(el, axis=1, keepdims=True)
    i1 = jnp.min(jnp.where(el == m1, colf, 1e9), axis=1, keepdims=True)
    el2 = jnp.where(colf == i1, NEG_BIG, el)
    m2 = jnp.max(el2, axis=1, keepdims=True)
    i2 = jnp.min(jnp.where(el2 == m2, colf, 1e9), axis=1, keepdims=True)
    r = jnp.exp(m2 - m1)
    w1 = pg / (1.0 + r)
    w2 = pg * r / (1.0 + r)
    return jnp.where(colf == i1, w1, 0.0) + jnp.where(colf == i2, w2, 0.0)


def _moe_prologue(x, sh2, sc2, wr_ref, br_ref):
    hn = _norm(x) * (1.0 + sc2) + sh2
    hi, lo = _split_hi_lo(hn)
    w = wr_ref[0]
    p = _bdot(hi, w)
    logits = p[:, 0:ROUTER_LANES] + p[:, ROUTER_LANES:] + _bdot(lo, w[:, 0:ROUTER_LANES]) + br_ref[0]
    return hi, _route(logits)


def _moe_kernel(x1_ref, mod_ref, wr_ref, br_ref, wup_ref, wdn_ref, g_ref, b_ref, o_ref, *, bt, tn):
    parts = [_moe_prologue(x1_ref[s], mod_ref[s, 3:4, :], mod_ref[s, 4:5, :], wr_ref, br_ref)
             for s in range(bt)]
    hn = parts[0][0] if bt == 1 else jnp.concatenate([p[0] for p in parts], axis=0)
    gate = parts[0][1] if bt == 1 else jnp.concatenate([p[1] for p in parts], axis=0)
    col = lax.broadcasted_iota(jnp.int32, gate.shape, 1)
    y = None
    for g in range(N_GROUPS):
        acts = []
        for e in range(g * EXPERTS_PER_GROUP, (g + 1) * EXPERTS_PER_GROUP):
            hid = _bdot(hn, wup_ref[0, e])
            a = hid[:, 0:D_EXPERT]
            u = hid[:, D_EXPERT:]
            gcol = jnp.sum(jnp.where(col == N_GROUPS + e, gate, 0.0), axis=1, keepdims=True)
            acts.append((a * (1.0 / (1.0 + jnp.exp(-a))) * u * gcol).astype(BF16))
        wdn = wdn_ref[0, g * EXPERTS_PER_GROUP:(g + 1) * EXPERTS_PER_GROUP]
        part = _bdot(jnp.concatenate(acts, axis=1), wdn.reshape(EXPERTS_PER_GROUP * D_EXPERT, D_MODEL))
        y = part if y is None else y + part
    for s in range(bt):
        g2 = mod_ref[s, 5:6, :]
        y2 = ALPHA * x1_ref[s] + (1.0 + g2) * y[s * tn:(s + 1) * tn]
        o_ref[s] = _norm(y2) * g_ref[0] + b_ref[0]


def _moe_call(x1, mod, w_r, b_r, w_up, w_down, ln_g, ln_b, layer, *, bt, tn):
    b, n, d = x1.shape
    tok = pl.BlockSpec((bt, tn, d), lambda bi, j: (bi, j, 0))
    per_layer = lambda a, **kw: pl.BlockSpec((1,) + a.shape[1:], lambda bi, j: (layer,) + (0,) * (a.ndim - 1), **kw)
    resident = dict(pipeline_mode=pl.Buffered(1))
    return pl.pallas_call(
        functools.partial(_moe_kernel, bt=bt, tn=tn),
        grid=(b // bt, n // tn),
        in_specs=[tok,
                  pl.BlockSpec((bt, 6, d), lambda bi, j: (bi, 0, 0)),
                  per_layer(w_r), per_layer(b_r),
                  per_layer(w_up, **resident), per_layer(w_down, **resident),
                  per_layer(ln_g), per_layer(ln_b)],
        out_specs=tok,
        out_shape=jax.ShapeDtypeStruct((b, n, d), F32),
        compiler_params=pltpu.CompilerParams(
            dimension_semantics=("arbitrary", "arbitrary"), vmem_limit_bytes=VMEM_LIMIT),
        name="moe",
    )(x1, mod, w_r, b_r, w_up, w_down, ln_g, ln_b)


def _pick(n, pref):
    t = min(n, pref)
    while n % t:
        t //= 2
    return t


def kernel(x_prompt, x_sample, c_prompt, c_sample, cache_k, cache_v, state_pool, w_mod, b_mod, w_in, w_pool, pool_scale, w_out, ln1_g, ln1_b, ln2_g, ln2_b, w_group, b_group, w_router, b_router, w_up, w_down):
    depth = w_mod.shape[0]
    bp, seq, d = x_prompt.shape
    bs, dseq, _ = x_sample.shape
    past = cache_k.shape[2]

    w_mod_b = w_mod.astype(BF16)
    w_in_b = w_in.astype(BF16)
    w_pool_b = w_pool.astype(BF16)
    w_out_b = w_out.astype(BF16)
    w_up_b = w_up.astype(BF16)
    w_down_b = w_down.astype(BF16)
    pad = jnp.zeros((depth, d, ROUTER_LANES - N_GROUPS - N_EXPERTS), F32)
    w_r32 = jnp.concatenate([w_group, w_router, pad], axis=-1)
    w_r_hi = w_r32.astype(BF16)
    w_r = jnp.concatenate([w_r_hi, (w_r32 - w_r_hi.astype(F32)).astype(BF16)], axis=-1)
    b_r = jnp.concatenate([b_group, b_router, pad[:, 0, :]], axis=-1).reshape(depth, 1, ROUTER_LANES)

    cache_kt = jnp.transpose(cache_k, (0, 1, 3, 4, 2))
    cache_vt = jnp.transpose(cache_v, (0, 1, 3, 4, 2))

    ln2_g3 = ln2_g.reshape(depth, 1, d)
    ln2_b3 = ln2_b.reshape(depth, 1, d)

    mod_all = _mod_call(jnp.concatenate([c_prompt, c_sample], axis=0), w_mod_b, b_mod)
    mod_p = mod_all[:, :bp].reshape(depth, bp, 6, d)
    mod_s = mod_all[:, bp:].reshape(depth, bs, 6, d)

    tm_p = _pick(seq, 1024)
    tq_p = _pick(seq, 512)
    tn_p = _pick(seq, 1024)
    bt_s = _pick(bs, max(1, 1024 // dseq))

    def run_layer(l, x, mod, hist16, pos0, is_prompt, kv_prev=None):
        inproj_args = (x, mod, hist16, w_in_b[l], w_pool_b[l], pool_scale[l].reshape(1, POOL_WIDTH))
        if is_prompt:
            q, k, v, kb, vb, po, new_pool = _inproj_call(*inproj_args, kv_prev, tm=tm_p, pos0=pos0)
        else:
            q, k, v, kb, vb, po, new_pool = _inproj_short_call(
                *inproj_args, bt=_pick(bs, max(1, INPROJ_CHUNK // dseq)), pos0=pos0)
        if is_prompt:
            at = _attn_prompt_call(q, kb, vb, tq=tq_p, ts=min(tq_p, 128))
            bt, tn = 1, tn_p
        else:
            at = _attn_sample_call(q, kb, vb, cache_kt, cache_vt, l, tk=256)
            bt, tn = bt_s, dseq
        x1 = _outproj_call(x, po, at, mod, w_out_b[l], ln1_g[l].reshape(1, d), ln1_b[l].reshape(1, d),
                           bt=bt, tn=tn)
        x2 = _moe_call(x1, mod, w_r, b_r, w_up_b, w_down_b, ln2_g3, ln2_b3, l,
                       bt=1 if is_prompt else _pick(bs, max(1, MOE_TILE // dseq)), tn=_pick(seq, MOE_TILE) if is_prompt else dseq)
        heads = lambda a: a.reshape(a.shape[:-1] + (SB_HEADS, SB_HEAD_DIM))
        return x2, heads(k), heads(v), new_pool[:, HIST_ROWS - POOL_HIST:]

    y = x_prompt
    pp = []
    kv_prev = None
    zero_hist = jnp.zeros((bp, HIST_ROWS, POOL_WIDTH), F32)
    for l in range(depth):
        y, k_all, v_all, p_new = run_layer(l, y, mod_p[l], zero_hist, 0, True, kv_prev)
        kv_prev = (k_all.reshape(-1, bp, seq, SB_WIDTH), v_all.reshape(-1, bp, seq, SB_WIDTH))
        pp.append(p_new)
    y_prompt = y
    k_prompt = k_all.reshape(depth, bp, seq, SB_HEADS, SB_HEAD_DIM)
    v_prompt = v_all.reshape(depth, bp, seq, SB_HEADS, SB_HEAD_DIM)

    y = x_sample
    kd, vd, pd = [], [], []
    hist_s = jnp.pad(state_pool, ((0, 0), (0, 0), (HIST_ROWS - POOL_HIST, 0), (0, 0)))
    for l in range(depth):
        y, k_new, v_new, p_new = run_layer(l, y, mod_s[l], hist_s[l], past, False)
        kd.append(k_new); vd.append(v_new); pd.append(p_new)
    y_sample = y

    return (y_prompt, y_sample, k_prompt, v_prompt, jnp.stack(pp),
            jnp.stack(kd), jnp.stack(vd), jnp.stack(pd))
```
